```python
import jax, jax.numpy as jnp
from jax import lax
import numpy as np

D_MODEL = 4096
BATCH = 8
SEQ = 2048
DEPTH = 2

CTX_LEN = 256
GRID_W = 64

ATT_HD = 128
ATT_HEADS = 12
KV_HEADS = 4
KV_GROUP = ATT_HEADS // KV_HEADS
WINDOW = 128
ATT_BLOCK = 128
ROPE_BASE = 10000.0
ROPE_PAIRS = ATT_HD // 4
NEG_INF = -1e30

RWKV_HD = 64
RWKV_HEADS = 24
RWKV_WIDTH = RWKV_HEADS * RWKV_HD
DECAY_LORA = 64
ICLR_LORA = 64
GATE_LORA = 224
RWKV_GN_EPS = 64e-5

CONV_CH = 1024
CONV_K = 31

Q_COLS = ATT_HEADS * ATT_HD
KV_COLS = KV_HEADS * ATT_HD
ATT_COLS = Q_COLS + 2 * KV_COLS
RWKV_COLS = 3 * RWKV_WIDTH + 2 * DECAY_LORA + 2 * ICLR_LORA + GATE_LORA
RWKV_SPLITS = (RWKV_WIDTH, 2 * RWKV_WIDTH, 3 * RWKV_WIDTH,
               3 * RWKV_WIDTH + 2 * DECAY_LORA, 3 * RWKV_WIDTH + 2 * DECAY_LORA + 2 * ICLR_LORA)
CONV_COLS = 2 * CONV_CH
IN_COLS = ATT_COLS + RWKV_COLS + CONV_COLS
D_MIX = Q_COLS + RWKV_WIDTH + CONV_CH

N_EXPERTS = 32
N_GROUPS = 4
EXPERTS_PER_GROUP = N_EXPERTS // N_GROUPS
TOP_K = 2
D_EXPERT = 640
MOE_BLOCK = 128

kernel_name = 'hybrid_dit_attn_rwkv7_conformer_gmoe'

F32 = jnp.float32


def rms_norm(x, g, eps=1e-6):
    xf = x.astype(F32)
    y = xf * lax.rsqrt(jnp.mean(xf * xf, axis=-1, keepdims=True) + eps)
    return (y * g.astype(F32)).astype(x.dtype)


def layer_norm(x, g, b, eps=1e-5):
    xf = x.astype(F32)
    mu = jnp.mean(xf, axis=-1, keepdims=True)
    var = jnp.mean(jnp.square(xf - mu), axis=-1, keepdims=True)
    return ((xf - mu) * lax.rsqrt(var + eps) * g.astype(F32) + b.astype(F32)).astype(x.dtype)


def modulate(h, shift, scale):
    return h * (1 + scale) + shift


def axial_rope_tables(rows):
    row = jnp.repeat(jnp.arange(rows), GRID_W).astype(F32)
    col = jnp.tile(jnp.arange(GRID_W), rows).astype(F32)
    inv = ROPE_BASE ** (-jnp.arange(ROPE_PAIRS, dtype=F32) / ROPE_PAIRS)
    ang = jnp.stack([row[:, None] * inv, col[:, None] * inv], axis=1)
    return jnp.cos(ang), jnp.sin(ang)


def apply_rope(x, cos, sin):
    b, n, h, d = x.shape
    xr = x.astype(F32).reshape(b, n, h, 2, 2, ROPE_PAIRS)
    x1, x2 = xr[..., 0, :], xr[..., 1, :]
    cb, sb = cos[None, :, None], sin[None, :, None]
    out = jnp.stack([x1 * cb - x2 * sb, x2 * cb + x1 * sb], axis=-2)
    return out.reshape(b, n, h, d).astype(x.dtype)


def latent_attention(ql, kl, vl, kc, vc, sink):
    b, s = ql.shape[:2]
    nb = s // ATT_BLOCK
    c = kc.shape[1]
    qb = ql.reshape(b, nb, ATT_BLOCK, KV_HEADS, KV_GROUP, ATT_HD)

    def band(t):
        tp = jnp.pad(t, ((0, 0), (ATT_BLOCK, ATT_BLOCK), (0, 0), (0, 0)))
        tp = tp.reshape(b, nb + 2, ATT_BLOCK, KV_HEADS, ATT_HD)
        return jnp.concatenate([tp[:, :-2], tp[:, 1:-1], tp[:, 2:]], axis=2)

    kb, vb = band(kl), band(vl)
    scale = ATT_HD ** -0.5
    s_band = jnp.einsum('bnqhgd,bnkhd->bnhgqk', qb, kb).astype(F32) * scale
    blk = jnp.arange(nb)[:, None, None]
    qpos = blk * ATT_BLOCK + jnp.arange(ATT_BLOCK)[None, :, None]
    kpos = (blk - 1) * ATT_BLOCK + jnp.arange(3 * ATT_BLOCK)[None, None, :]
    valid = (jnp.abs(kpos - qpos) <= WINDOW) & (kpos >= 0) & (kpos < s)
    s_band = jnp.where(valid[None, :, None, None], s_band, NEG_INF)
    s_ctx = jnp.einsum('bnqhgd,bchd->bnhgqc', qb, kc).astype(F32) * scale
    sk = jnp.broadcast_to(sink.astype(F32).reshape(1, 1, KV_HEADS, KV_GROUP, 1, 1), s_band.shape[:-1] + (1,))
    p = jax.nn.softmax(jnp.concatenate([s_band, s_ctx, sk], axis=-1), axis=-1)
    nk = 3 * ATT_BLOCK
    o = (jnp.einsum('bnhgqk,bnkhd->bnqhgd', p[..., :nk].astype(vb.dtype), vb)
         + jnp.einsum('bnhgqc,bchd->bnqhgd', p[..., nk:nk + c].astype(vc.dtype), vc))
    return o.reshape(b, s, Q_COLS)


def context_attention(qc, kc, vc, sink):
    b, c = qc.shape[:2]
    qg = qc.reshape(b, c, KV_HEADS, KV_GROUP, ATT_HD)
    s = jnp.einsum('bqhgd,bkhd->bhgqk', qg, kc).astype(F32) * ATT_HD ** -0.5
    sk = jnp.broadcast_to(sink.astype(F32).reshape(1, KV_HEADS, KV_GROUP, 1, 1), s.shape[:-1] + (1,))
    p = jax.nn.softmax(jnp.concatenate([s, sk], axis=-1), axis=-1)[..., :c]
    o = jnp.einsum('bhgqk,bkhd->bqhgd', p.astype(vc.dtype), vc)
    return o.reshape(b, c, Q_COLS)


def attention_mixer(zc, zl, prm, cos, sin, ctx_out):
    def split_heads(z):
        b, t = z.shape[:2]
        q, k, v = jnp.split(z, [Q_COLS, Q_COLS + KV_COLS], axis=-1)
        q = rms_norm(q.reshape(b, t, ATT_HEADS, ATT_HD), prm['q_norm_g'])
        k = rms_norm(k.reshape(b, t, KV_HEADS, ATT_HD), prm['k_norm_g'])
        return q, k, v.reshape(b, t, KV_HEADS, ATT_HD)

    qc, kc, vc = split_heads(zc)
    ql, kl, vl = split_heads(zl)
    out_l = latent_attention(apply_rope(ql, cos, sin), apply_rope(kl, cos, sin), vl, kc, vc, prm['attn_sink'])
    out_c = context_attention(qc, kc, vc, prm['attn_sink']) if ctx_out else None
    return out_l, out_c


def token_shift(z, mu_prev, mu_next):
    prev = jnp.pad(z[:, :-1], ((0, 0), (1, 0), (0, 0)))
    nxt = jnp.pad(z[:, 1:], ((0, 0), (0, 1), (0, 0)))
    return z + (prev - z) * mu_prev + (nxt - z) * mu_next


def rwkv_prepare(z, prm):
    b, t = z.shape[:2]
    z = token_shift(z, prm['rwkv_mu_prev'], prm['rwkv_mu_next'])
    r, k, v, wd, ad, gd = jnp.split(z, RWKV_SPLITS, axis=-1)
    wd = wd.reshape(b, t, 2, DECAY_LORA)
    ad = ad.reshape(b, t, 2, ICLR_LORA)
    w_pre = (prm['rwkv_w0'] + jnp.einsum('btdr,drc->btdc', jnp.tanh(wd), prm['rwkv_w2'])).astype(F32)
    decay = jnp.exp(-jnp.exp(-jax.nn.softplus(-w_pre) - 0.5))
    a = jax.nn.sigmoid((prm['rwkv_a0'] + jnp.einsum('btdr,drc->btdc', ad, prm['rwkv_a2'])).astype(F32))
    kk = (k * prm['rwkv_k_k']).astype(F32).reshape(b, t, RWKV_HEADS, RWKV_HD)
    kk = kk / jnp.maximum(jnp.sqrt(jnp.sum(kk * kk, axis=-1, keepdims=True)), 1e-12)
    kmod = k[:, :, None].astype(F32) * (1 + (a - 1) * prm['rwkv_k_a'].astype(F32))
    g = jax.nn.sigmoid(gd) @ prm['rwkv_g2']
    heads = lambda u: u.reshape(u.shape[:-1] + (RWKV_HEADS, RWKV_HD))
    return {'r': heads(r.astype(F32)), 'v': heads(v.astype(F32)), 'g': g, 'kk': kk,
            'decay': heads(decay), 'a': heads(a), 'kmod': heads(kmod)}


def wkv_scan(state0, p, d, reverse):
    xs = tuple(jnp.moveaxis(u, 1, 0) for u in
               (p['r'], p['decay'][:, :, d], p['kmod'][:, :, d], p['v'], p['kk'], p['a'][:, :, d]))

    def step(S, inp):
        r_t, w_t, k_t, v_t, kk_t, a_t = inp
        s_kk = jnp.einsum('bhvk,bhk->bhv', S, kk_t)
        S = (S * w_t[:, :, None, :] - s_kk[..., None] * (kk_t * a_t)[:, :, None, :]
             + v_t[..., None] * k_t[:, :, None, :])
        return S, jnp.einsum('bhvk,bhk->bhv', S, r_t)

    S, y = lax.scan(step, state0, xs, reverse=reverse)
    return S, jnp.moveaxis(y, 0, 1)


def rwkv_output(p, y, prm):
    b, t = y.shape[:2]
    mu = jnp.mean(y, axis=-1, keepdims=True)
    var = jnp.mean(jnp.square(y - mu), axis=-1, keepdims=True)
    yn = ((y - mu) * lax.rsqrt(var + RWKV_GN_EPS)).reshape(b, t, RWKV_WIDTH)
    yn = yn * prm['rwkv_ln_g'].astype(F32) + prm['rwkv_ln_b'].astype(F32)
    bonus = jnp.sum(p['r'] * (p['kmod'][:, :, 0] + p['kmod'][:, :, 1]) * prm['rwkv_r_k'].astype(F32),
                    axis=-1, keepdims=True) * p['v']
    return (yn + bonus.reshape(b, t, RWKV_WIDTH)) * p['g'].astype(F32)


def rwkv_mixer(zc, zl, prm, ctx_out):
    pc, pl = rwkv_prepare(zc, prm), rwkv_prepare(zl, prm)
    s0 = jnp.zeros((zl.shape[0], RWKV_HEADS, RWKV_HD, RWKV_HD), F32)
    yc_dirs, yl_dirs = [], []
    for d, rev in ((0, False), (1, True)):
        s_ctx, yc = wkv_scan(s0, pc, d, rev)
        _, yl = wkv_scan(s_ctx, pl, d, rev)
        yc_dirs.append(yc)
        yl_dirs.append(yl)
    out_l = rwkv_output(pl, yl_dirs[0] + yl_dirs[1], prm)
    out_c = rwkv_output(pc, yc_dirs[0] + yc_dirs[1], prm) if ctx_out else None
    return out_l, out_c


def conformer_conv(z, prm):
    val, gate = jnp.split(z, 2, axis=-1)
    u = val * jax.nn.sigmoid(gate)
    u = lax.conv_general_dilated(u, prm['conv_w'][:, None, :], window_strides=(1,),
                                 padding=((CONV_K // 2, CONV_K // 2),),
                                 dimension_numbers=('NWC', 'WIO', 'NWC'),
                                 feature_group_count=CONV_CH) + prm['conv_b']
    return jax.nn.silu(layer_norm(u, prm['conv_ln_g'], prm['conv_ln_b']))


def moe_ffn(h, router_w, router_b, w_gate, w_up, w_down):
    n = h.shape[0]
    scores = jax.nn.sigmoid(h.astype(F32) @ router_w.astype(F32))
    sel = (scores + router_b.astype(F32)).reshape(n, N_GROUPS, EXPERTS_PER_GROUP)
    grp_score = jnp.sum(lax.top_k(sel, TOP_K)[0], axis=-1)
    g_idx = jnp.argmax(grp_score, axis=-1)
    in_grp = jnp.take_along_axis(sel, g_idx[:, None, None], axis=1)[:, 0]
    _, local = lax.top_k(in_grp, TOP_K)
    expert = g_idx[:, None] * EXPERTS_PER_GROUP + local
    wts = jnp.take_along_axis(scores, expert, axis=1)
    wts = wts / jnp.sum(wts, axis=-1, keepdims=True)

    nk = n * TOP_K
    e_flat = expert.reshape(-1)
    tok = jnp.repeat(jnp.arange(n, dtype=jnp.int32), TOP_K)
    order = jnp.argsort(e_flat)
    e_s, tok_s, wt_s = e_flat[order], tok[order], wts.reshape(-1)[order]
    counts = jnp.bincount(e_flat, length=N_EXPERTS)
    start = jnp.cumsum(counts) - counts
    padded = (counts + MOE_BLOCK - 1) // MOE_BLOCK * MOE_BLOCK
    pend = jnp.cumsum(padded)
    pstart = pend - padded
    dest = pstart[e_s] + jnp.arange(nk) - start[e_s]
    n_blocks = -(-(nk + N_EXPERTS * (MOE_BLOCK - 1)) // MOE_BLOCK)
    slots = n_blocks * MOE_BLOCK
    slot_tok = jnp.zeros((slots,), jnp.int32).at[dest].set(tok_s)
    slot_wt = jnp.zeros((slots,), F32).at[dest].set(wt_s)
    blk_expert = jnp.clip(jnp.searchsorted(pend, jnp.arange(n_blocks) * MOE_BLOCK, side='right'),
                          0, N_EXPERTS - 1)

    def body(y, blk):
        toks, wb, e = blk
        xb = h[toks]
        hid = jax.nn.silu(xb @ w_gate[e]) * (xb @ w_up[e])
        return y.at[toks].add((hid @ w_down[e]) * wb.astype(h.dtype)[:, None]), None

    y, _ = lax.scan(body, jnp.zeros_like(h),
                    (slot_tok.reshape(n_blocks, MOE_BLOCK), slot_wt.reshape(n_blocks, MOE_BLOCK), blk_expert))
    return y


def setup_inputs(seed: int = 0) -> dict:
    key = jax.random.key(seed)
    ks = iter(jax.random.split(key, 40))
    nrm = lambda shape, s: jax.random.normal(next(ks), shape, F32) * s
    L, D = DEPTH, D_MODEL
    return {
        'x': nrm((BATCH, SEQ, D), 1.0),
        'c': nrm((BATCH, D), 1.0),
        'ctx': nrm((BATCH, CTX_LEN, D), 1.0),
        'c_ctx': nrm((D,), 1.0),
        'ada_w': nrm((L, D, 6 * D), 0.5 * D ** -0.5),
        'ada_b': nrm((L, 6 * D), 0.02),
        'norm1_g': 1.0 + nrm((L, D), 0.05),
        'norm2_g': 1.0 + nrm((L, D), 0.05),
        'w_in': nrm((L, D, IN_COLS), D ** -0.5),
        'q_norm_g': 1.0 + nrm((L, ATT_HD), 0.05),
        'k_norm_g': 1.0 + nrm((L, ATT_HD), 0.05),
        'attn_sink': nrm((L, ATT_HEADS), 1.0),
        'rwkv_mu_prev': 0.3 + nrm((L, RWKV_COLS), 0.05),
        'rwkv_mu_next': 0.3 + nrm((L, RWKV_COLS), 0.05),
        'rwkv_w0': jnp.linspace(-6.0, -1.0, RWKV_WIDTH, dtype=F32) + nrm((L, 2, RWKV_WIDTH), 0.1),
        'rwkv_w2': nrm((L, 2, DECAY_LORA, RWKV_WIDTH), 0.1 * DECAY_LORA ** -0.5),
        'rwkv_a0': nrm((L, 2, RWKV_WIDTH), 0.3),
        'rwkv_a2': nrm((L, 2, ICLR_LORA, RWKV_WIDTH), 0.3 * ICLR_LORA ** -0.5),
        'rwkv_g2': nrm((L, GATE_LORA, RWKV_WIDTH), GATE_LORA ** -0.5),
        'rwkv_k_k': 0.85 + nrm((L, RWKV_WIDTH), 0.05),
        'rwkv_k_a': 1.0 + nrm((L, RWKV_WIDTH), 0.05),
        'rwkv_r_k': nrm((L, RWKV_HEADS, RWKV_HD), 0.1),
        'rwkv_ln_g': 1.0 + nrm((L, RWKV_WIDTH), 0.05),
        'rwkv_ln_b': nrm((L, RWKV_WIDTH), 0.02),
        'conv_w': nrm((L, CONV_K, CONV_CH), CONV_K ** -0.5),
        'conv_b': nrm((L, CONV_CH), 0.02),
        'conv_ln_g': 1.0 + nrm((L, CONV_CH), 0.05),
        'conv_ln_b': nrm((L, CONV_CH), 0.02),
        'w_out': nrm((L, D_MIX, D), D_MIX ** -0.5),
        'router_w': nrm((D, N_EXPERTS), D ** -0.5),
        'router_b': nrm((N_EXPERTS,), 0.01),
        'moe_w_gate': nrm((L, N_EXPERTS, D, D_EXPERT), D ** -0.5),
        'moe_w_up': nrm((L, N_EXPERTS, D, D_EXPERT), D ** -0.5),
        'moe_w_down': nrm((L, N_EXPERTS, D_EXPERT, D), D_EXPERT ** -0.5),
    }


def reference(x, c, ctx, c_ctx, ada_w, ada_b, norm1_g, norm2_g, w_in, q_norm_g, k_norm_g, attn_sink,
              rwkv_mu_prev, rwkv_mu_next, rwkv_w0, rwkv_w2, rwkv_a0, rwkv_a2, rwkv_g2, rwkv_k_k, rwkv_k_a,
              rwkv_r_k, rwkv_ln_g, rwkv_ln_b, conv_w, conv_b, conv_ln_g, conv_ln_b, w_out, router_w, router_b,
              moe_w_gate, moe_w_up, moe_w_down):
    dt = x.dtype
    rows = x.shape[1] // GRID_W
    cos, sin = axial_rope_tables(rows)
    s_c = jax.nn.silu(c)
    s_cc = jax.nn.silu(c_ctx)
    xl, xc = x, ctx
    for l in range(DEPTH):
        ctx_out = l < DEPTH - 1
        prm = {'q_norm_g': q_norm_g[l], 'k_norm_g': k_norm_g[l], 'attn_sink': attn_sink[l],
               'rwkv_mu_prev': rwkv_mu_prev[l], 'rwkv_mu_next': rwkv_mu_next[l],
               'rwkv_w0': rwkv_w0[l], 'rwkv_w2': rwkv_w2[l], 'rwkv_a0': rwkv_a0[l], 'rwkv_a2': rwkv_a2[l],
               'rwkv_g2': rwkv_g2[l], 'rwkv_k_k': rwkv_k_k[l], 'rwkv_k_a': rwkv_k_a[l], 'rwkv_r_k': rwkv_r_k[l],
               'rwkv_ln_g': rwkv_ln_g[l], 'rwkv_ln_b': rwkv_ln_b[l], 'conv_w': conv_w[l], 'conv_b': conv_b[l],
               'conv_ln_g': conv_ln_g[l], 'conv_ln_b': conv_ln_b[l]}
        sh1, sc1, g1, sh2, sc2, g2 = [m[:, None] for m in jnp.split(s_c @ ada_w[l] + ada_b[l], 6, axis=-1)]
        sh1c, sc1c, g1c, sh2c, sc2c, g2c = jnp.split(s_cc @ ada_w[l] + ada_b[l], 6, axis=-1)

        zl = modulate(rms_norm(xl, norm1_g[l]), sh1, sc1) @ w_in[l]
        zc = modulate(rms_norm(xc, norm1_g[l]), sh1c, sc1c) @ w_in[l]
        att_l, rw_l, cv_l = jnp.split(zl, [ATT_COLS, ATT_COLS + RWKV_COLS], axis=-1)
        att_c, rw_c, cv_c = jnp.split(zc, [ATT_COLS, ATT_COLS + RWKV_COLS], axis=-1)
        a_l, a_c = attention_mixer(att_c, att_l, prm, cos, sin, ctx_out)
        r_l, r_c = rwkv_mixer(rw_c, rw_l, prm, ctx_out)
        c_l = conformer_conv(cv_l, prm)
        mix_l = jnp.concatenate([a_l.astype(dt), r_l.astype(dt), c_l.astype(dt)], axis=-1)
        xl = xl + g1 * (mix_l @ w_out[l])
        h2l = modulate(rms_norm(xl, norm2_g[l]), sh2, sc2)

        if ctx_out:
            c_c = conformer_conv(cv_c, prm)
            mix_c = jnp.concatenate([a_c.astype(dt), r_c.astype(dt), c_c.astype(dt)], axis=-1)
            xc = xc + g1c * (mix_c @ w_out[l])
            h2c = modulate(rms_norm(xc, norm2_g[l]), sh2c, sc2c)
            n_c = h2c.shape[0] * h2c.shape[1]
            tokens = jnp.concatenate([h2c.reshape(-1, D_MODEL), h2l.reshape(-1, D_MODEL)], axis=0)
            y = moe_ffn(tokens, router_w, router_b, moe_w_gate[l], moe_w_up[l], moe_w_down[l])
            xc = xc + g2c * y[:n_c].reshape(xc.shape)
            xl = xl + g2 * y[n_c:].reshape(xl.shape)
        else:
            y = moe_ffn(h2l.reshape(-1, D_MODEL), router_w, router_b, moe_w_gate[l], moe_w_up[l], moe_w_down[l])
            xl = xl + g2 * y.reshape(xl.shape)
    return xl
```

```python
import functools
import math
from typing import NamedTuple

import jax
import jax.numpy as jnp
from jax import lax
from jax.experimental import pallas as pl
from jax.experimental.pallas import tpu as pltpu

F32 = jnp.float32
BF = jnp.bfloat16

D_MODEL = 4096
DEPTH = 2
GRID_W = 64
ATT_HD = 128
ATT_HEADS = 12
KV_HEADS = 4
KV_GROUP = ATT_HEADS // KV_HEADS
ATT_BLOCK = 128
ROPE_BASE = 10000.0
ROPE_PAIRS = ATT_HD // 4
NEG_INF = -1e30
RWKV_HD = 64
RWKV_HEADS = 24
RWKV_WIDTH = RWKV_HEADS * RWKV_HD
DECAY_LORA = 64
ICLR_LORA = 64
GATE_LORA = 224
RWKV_GN_EPS = 64e-5
CONV_CH = 1024
CONV_K = 31
Q_COLS = ATT_HEADS * ATT_HD
KV_COLS = KV_HEADS * ATT_HD
ATT_COLS = Q_COLS + 2 * KV_COLS
RWKV_COLS = 3 * RWKV_WIDTH + 2 * DECAY_LORA + 2 * ICLR_LORA + GATE_LORA
CONV_COLS = 2 * CONV_CH
N_EXPERTS = 32
N_GROUPS = 4
EXPERTS_PER_GROUP = N_EXPERTS // N_GROUPS
TOP_K = 2
D_EXPERT = 640

LANES = 128
SUBLANES_BF16 = 16
VMEM_LIMIT = 56 * 1024 * 1024

RWKV_PAD = 5120
GATE_PAD = RWKV_PAD - (3 * RWKV_WIDTH + 2 * DECAY_LORA + 2 * ICLR_LORA)
Z_R, Z_K, Z_V = 0, RWKV_WIDTH, 2 * RWKV_WIDTH
Z_WD = 3 * RWKV_WIDTH
Z_AD = Z_WD + 2 * DECAY_LORA
Z_GD = Z_AD + 2 * ICLR_LORA
Z_CONV = RWKV_PAD
Z_ATT = RWKV_PAD + CONV_COLS
Z_COLS = Z_ATT + ATT_COLS

SCAN_CHUNK = 64
SCAN_SUB = 16
SCAN_HEADS = 4
MOE_ROWS = 256
COMBINE_ROWS = 128


class Dims(NamedTuple):
    nb: int
    s: int
    cl: int

    @property
    def r_ctx(self):
        return self.nb * self.cl

    @property
    def r_lat(self):
        return self.nb * self.s

    @property
    def rows(self):
        return self.r_ctx + self.r_lat


def _cparams(n_axes):
    return pltpu.CompilerParams(dimension_semantics=("arbitrary",) * n_axes, vmem_limit_bytes=VMEM_LIMIT)


def _row_tile(pref, dims, within_seq=False):
    t = math.gcd(pref, math.gcd(dims.r_ctx, dims.s))
    if within_seq:
        t = math.gcd(t, dims.cl)
    return t


def _grp(i, tr, dims):
    nctx = dims.r_ctx // tr
    per = dims.s // tr
    return jnp.where(i < nctx, dims.nb, (i - nctx) // per)


def _seq_pos(i, tt, dims):
    nctx = dims.r_ctx // tt
    pc, pl_ = dims.cl // tt, dims.s // tt
    is_ctx = i < nctx
    pos = jnp.where(is_ctx, i % pc, (i - nctx) % pl_)
    cnt = jnp.where(is_ctx, pc, pl_)
    return pos, cnt


def _sigmoid(x):
    return 1.0 / (1.0 + jnp.exp(-x))


def _dot(a, b):
    return jnp.dot(a, b, preferred_element_type=F32)


def _dot_nt(a, b):
    return lax.dot_general(a, b, (((1,), (1,)), ((), ())), preferred_element_type=F32)


def _dot_tn(a, b):
    return lax.dot_general(a, b, (((0,), (0,)), ((), ())), preferred_element_type=F32)


def _split3(x):
    hi = x.astype(BF)
    r1 = x - hi.astype(F32)
    mid = r1.astype(BF)
    lo = (r1 - mid.astype(F32)).astype(BF)
    return hi, mid, lo


def _dot_exact_lhs(m_bf, x):
    hi, mid, lo = _split3(x)
    return _dot(m_bf, hi) + _dot(m_bf, mid) + _dot(m_bf, lo)


def _dot_exact_rhs(x, m_bf):
    hi, mid, lo = _split3(x)
    return _dot(hi, m_bf) + _dot(mid, m_bf) + _dot(lo, m_bf)


def _dot_hp(a, b):
    ah = a.astype(BF)
    al = (a - ah.astype(F32)).astype(BF)
    bh = b.astype(BF)
    bl = (b - bh.astype(F32)).astype(BF)
    return _dot(ah, bh) + (_dot(ah, bl) + _dot(al, bh))


def _seg_ones(width):
    i = lax.broadcasted_iota(jnp.int32, (LANES, LANES), 0) // width
    j = lax.broadcasted_iota(jnp.int32, (LANES, LANES), 1) // width
    return jnp.where(i == j, 1.0, 0.0).astype(BF)


def _seg_sum(x, seg):
    parts = [_dot_exact_rhs(x[:, j:j + LANES], seg) for j in range(0, x.shape[1], LANES)]
    return parts[0] if len(parts) == 1 else jnp.concatenate(parts, axis=1)


def _ada_kernel(c_ref, w_ref, b_ref, o_ref):
    cv = c_ref[...]
    act = (cv * _sigmoid(cv)).astype(BF)
    o_ref[0] = _dot(act, w_ref[0].astype(BF)) + b_ref[0]


def ada_tables(cs, ada_w, ada_b):
    n_layers, d, n = ada_w.shape
    tn = 512
    return pl.pallas_call(
        _ada_kernel,
        grid=(n_layers, n // tn),
        in_specs=[pl.BlockSpec((cs.shape[0], d), lambda l, j: (0, 0)),
                  pl.BlockSpec((1, d, tn), lambda l, j: (l, 0, j)),
                  pl.BlockSpec((1, 1, tn), lambda l, j: (l, 0, j))],
        out_specs=pl.BlockSpec((1, cs.shape[0], tn), lambda l, j: (l, 0, j)),
        out_shape=jax.ShapeDtypeStruct((n_layers, cs.shape[0], n), F32),
        compiler_params=_cparams(2),
        name="ada_tables",
    )(cs, ada_w, ada_b.reshape(n_layers, 1, n))


def _norm_mod_kernel(x_ref, g_ref, sh_ref, sc_ref, *rest, with_router):
    x = x_ref[...]
    ms = jnp.mean(x * x, axis=-1, keepdims=True)
    y = x * lax.rsqrt(ms + 1e-6) * g_ref[...]
    h = y * (1.0 + sc_ref[0]) + sh_ref[0]
    if with_router:
        rw_ref, h_ref, lg_ref = rest
        h_ref[...] = h
        lg_ref[...] = jnp.dot(h, rw_ref[...], precision=lax.Precision.HIGHEST, preferred_element_type=F32)
    else:
        (h_ref,) = rest
        h_ref[...] = h.astype(BF)


def norm_mod(x, g, sh, sc, dims, row0=0, router_w=None):
    d = x.shape[1]
    tr = _row_tile(256, dims)
    b0 = row0 // tr
    nblk = (x.shape[0] - row0) // tr
    in_specs = [pl.BlockSpec((tr, d), lambda i: (i + b0, 0)),
                pl.BlockSpec((1, d), lambda i: (0, 0)),
                pl.BlockSpec((1, 1, d), lambda i: (_grp(i + b0, tr, dims), 0, 0)),
                pl.BlockSpec((1, 1, d), lambda i: (_grp(i + b0, tr, dims), 0, 0))]
    args = [x, g.reshape(1, d), sh, sc]
    if router_w is None:
        out_specs = pl.BlockSpec((tr, d), lambda i: (i, 0))
        out_shape = jax.ShapeDtypeStruct((nblk * tr, d), BF)
    else:
        in_specs.append(pl.BlockSpec((d, LANES), lambda i: (0, 0)))
        args.append(router_w)
        out_specs = [pl.BlockSpec((tr, d), lambda i: (i, 0)), pl.BlockSpec((tr, LANES), lambda i: (i, 0))]
        out_shape = [jax.ShapeDtypeStruct((nblk * tr, d), F32), jax.ShapeDtypeStruct((nblk * tr, LANES), F32)]
    return pl.pallas_call(
        functools.partial(_norm_mod_kernel, with_router=router_w is not None),
        grid=(nblk,), in_specs=in_specs, out_specs=out_specs, out_shape=out_shape,
        compiler_params=_cparams(1),
        name="norm_mod_router" if router_w is not None else "norm_mod",
    )(*args)


def _mm_kernel(a_ref, b_ref, o_ref):
    o_ref[...] = _dot(a_ref[...], b_ref[...]).astype(o_ref.dtype)


def in_proj(h, w, dims):
    m, k = h.shape
    n = w.shape[1]
    tm = _row_tile(1024, dims)
    tn = 512
    return pl.pallas_call(
        _mm_kernel,
        grid=(m // tm, n // tn),
        in_specs=[pl.BlockSpec((tm, k), lambda i, j: (i, 0)),
                  pl.BlockSpec((k, tn), lambda i, j: (0, j))],
        out_specs=pl.BlockSpec((tm, tn), lambda i, j: (i, j)),
        out_shape=jax.ShapeDtypeStruct((m, n), BF),
        compiler_params=_cparams(2),
        name="in_proj",
    )(h, w)


def _qk_prep_kernel(z_ref, g_ref, cos_ref, sin_ref, o_ref):
    x = z_ref[...].astype(F32)
    ms = jnp.mean(x * x, axis=-1, keepdims=True)
    y = x * lax.rsqrt(ms + 1e-6) * g_ref[0]
    lane = lax.broadcasted_iota(jnp.int32, y.shape, 1)
    first_half = (lane % (2 * ROPE_PAIRS)) < ROPE_PAIRS
    partner = jnp.where(first_half, pltpu.roll(y, LANES - ROPE_PAIRS, 1), pltpu.roll(y, ROPE_PAIRS, 1))
    o_ref[...] = (y * cos_ref[...] + partner * sin_ref[...]).astype(BF)


def qk_prep(z, gains, cos_t, sin_t, dims):
    rows = z.shape[0]
    tr = _row_tile(256, dims, within_seq=True)
    nh = ATT_HEADS + KV_HEADS
    nctx = dims.r_ctx // tr
    pc, pl_ = dims.cl // tr, dims.s // tr

    def tab(i, h):
        return (jnp.where(i < nctx, i % pc, pc + (i - nctx) % pl_), 0)

    return pl.pallas_call(
        _qk_prep_kernel,
        grid=(rows // tr, nh),
        in_specs=[pl.BlockSpec((tr, ATT_HD), lambda i, h: (i, Z_ATT // ATT_HD + h)),
                  pl.BlockSpec((1, 1, ATT_HD), lambda i, h: (h, 0, 0)),
                  pl.BlockSpec((tr, ATT_HD), tab),
                  pl.BlockSpec((tr, ATT_HD), tab)],
        out_specs=pl.BlockSpec((tr, ATT_HD), lambda i, h: (i, h)),
        out_shape=jax.ShapeDtypeStruct((rows, nh * ATT_HD), BF),
        compiler_params=_cparams(2),
        name="qk_prep",
    )(z, gains, cos_t, sin_t)


def _stack_heads(q):
    return jnp.concatenate([q[:, g * ATT_HD:(g + 1) * ATT_HD] for g in range(KV_GROUP)], axis=0)


def _attn_lat_kernel(q_ref, kp_ref, kc_ref, kn_ref, kx_ref, vp_ref, vc_ref, vn_ref, vx_ref, sink_ref, o_ref):
    n = pl.program_id(2)
    last = pl.num_programs(2) - 1
    q3 = _stack_heads(q_ref[...])
    s_p = _dot_nt(q3, kp_ref[...])
    s_c = _dot_nt(q3, kc_ref[...])
    s_n = _dot_nt(q3, kn_ref[...])
    s_x = _dot_nt(q3, kx_ref[...])
    i = lax.broadcasted_iota(jnp.int32, s_p.shape, 0) % ATT_BLOCK
    j = lax.broadcasted_iota(jnp.int32, s_p.shape, 1)
    s_p = jnp.where((j >= i) & (n > 0), s_p, NEG_INF)
    s_n = jnp.where((j <= i) & (n < last), s_n, NEG_INF)
    sk = sink_ref[0]
    m = jnp.maximum(jnp.maximum(jnp.max(s_p, axis=-1, keepdims=True), jnp.max(s_c, axis=-1, keepdims=True)),
                    jnp.maximum(jnp.max(s_n, axis=-1, keepdims=True), jnp.max(s_x, axis=-1, keepdims=True)))
    m = jnp.maximum(m, sk)
    p_p, p_c, p_n, p_x = jnp.exp(s_p - m), jnp.exp(s_c - m), jnp.exp(s_n - m), jnp.exp(s_x - m)
    den = (jnp.sum(p_p, axis=-1, keepdims=True) + jnp.sum(p_c, axis=-1, keepdims=True)
           + jnp.sum(p_n, axis=-1, keepdims=True) + jnp.sum(p_x, axis=-1, keepdims=True) + jnp.exp(sk - m))
    o = (_dot(p_p.astype(BF), vp_ref[...]) + _dot(p_c.astype(BF), vc_ref[...])
         + _dot(p_n.astype(BF), vn_ref[...]) + _dot(p_x.astype(BF), vx_ref[...])) / den
    for g in range(KV_GROUP):
        o_ref[:, g * ATT_HD:(g + 1) * ATT_HD] = o[g * ATT_BLOCK:(g + 1) * ATT_BLOCK].astype(BF)


def attn_latent(qk, z, sink_rows, dims):
    nblk = dims.s // ATT_BLOCK
    lat0 = dims.r_ctx // ATT_BLOCK
    kcol = Q_COLS // ATT_HD
    vcol = (Z_ATT + Q_COLS + KV_COLS) // ATT_HD

    def row(b, n):
        return lat0 + b * nblk + n

    def kv_specs(col0):
        return [pl.BlockSpec((ATT_BLOCK, ATT_HD), lambda b, g, n: (row(b, jnp.maximum(n - 1, 0)), col0 + g)),
                pl.BlockSpec((ATT_BLOCK, ATT_HD), lambda b, g, n: (row(b, n), col0 + g)),
                pl.BlockSpec((ATT_BLOCK, ATT_HD), lambda b, g, n: (row(b, jnp.minimum(n + 1, nblk - 1)), col0 + g)),
                pl.BlockSpec((dims.cl, ATT_HD), lambda b, g, n: (b, col0 + g))]

    gw = KV_GROUP * ATT_HD
    return pl.pallas_call(
        _attn_lat_kernel,
        grid=(dims.nb, KV_HEADS, nblk),
        in_specs=([pl.BlockSpec((ATT_BLOCK, gw), lambda b, g, n: (row(b, n), g))]
                  + kv_specs(kcol) + kv_specs(vcol)
                  + [pl.BlockSpec((1, KV_GROUP * ATT_BLOCK, 1), lambda b, g, n: (g, 0, 0))]),
        out_specs=pl.BlockSpec((ATT_BLOCK, gw), lambda b, g, n: (b * nblk + n, g)),
        out_shape=jax.ShapeDtypeStruct((dims.r_lat, Q_COLS), BF),
        compiler_params=_cparams(3),
        name="attn_latent",
    )(qk, qk, qk, qk, qk, z, z, z, z, sink_rows)


def _attn_ctx_kernel(q_ref, k_ref, v_ref, sink_ref, o_ref):
    q3 = _stack_heads(q_ref[...])
    s = _dot_nt(q3, k_ref[...])
    sk = sink_ref[0]
    m = jnp.maximum(jnp.max(s, axis=-1, keepdims=True), sk)
    p = jnp.exp(s - m)
    den = jnp.sum(p, axis=-1, keepdims=True) + jnp.exp(sk - m)
    o = _dot(p.astype(BF), v_ref[...]) / den
    cl = q_ref.shape[0]
    for g in range(KV_GROUP):
        o_ref[:, g * ATT_HD:(g + 1) * ATT_HD] = o[g * cl:(g + 1) * cl].astype(BF)


def attn_context(qk, z, sink_rows, dims):
    kcol = Q_COLS // ATT_HD
    vcol = (Z_ATT + Q_COLS + KV_COLS) // ATT_HD
    gw = KV_GROUP * ATT_HD
    return pl.pallas_call(
        _attn_ctx_kernel,
        grid=(dims.nb, KV_HEADS),
        in_specs=[pl.BlockSpec((dims.cl, gw), lambda b, g: (b, g)),
                  pl.BlockSpec((dims.cl, ATT_HD), lambda b, g: (b, kcol + g)),
                  pl.BlockSpec((dims.cl, ATT_HD), lambda b, g: (b, vcol + g)),
                  pl.BlockSpec((1, KV_GROUP * dims.cl, 1), lambda b, g: (g, 0, 0))],
        out_specs=pl.BlockSpec((dims.cl, gw), lambda b, g: (b, g)),
        out_shape=jax.ShapeDtypeStruct((dims.r_ctx, Q_COLS), BF),
        compiler_params=_cparams(2),
        name="attn_context",
    )(qk, z, z, sink_rows)


def _shift_rows(zf, prev_row, next_row):
    tt = zf.shape[0]
    row = lax.broadcasted_iota(jnp.int32, zf.shape, 0)
    prev = jnp.where(row == 0, prev_row, pltpu.roll(zf, 1, 0))
    nxt = jnp.where(row == tt - 1, next_row, pltpu.roll(zf, tt - 1, 0))
    return prev, nxt


def _rwkv_prep_kernel(z_ref, zp_ref, zn_ref, mup_ref, mun_ref, w0_ref, w2_ref, a0_ref, a2_ref, g2_ref, kk_ref,
                      r_o, k_o, v_o, kk_o, a0_o, a1_o, lw0_o, lw1_o, g_o, *, dims):
    i = pl.program_id(0)
    tt = z_ref.shape[0]
    pos, cnt = _seq_pos(i, tt, dims)
    zf = z_ref[...].astype(F32)
    hb = zp_ref.shape[0]
    prev_row = jnp.where(pos > 0, zp_ref[...].astype(F32)[hb - 1:hb, :], 0.0)
    next_row = jnp.where(pos < cnt - 1, zn_ref[...].astype(F32)[0:1, :], 0.0)
    prev, nxt = _shift_rows(zf, prev_row, next_row)
    zs = zf + (prev - zf) * mup_ref[...] + (nxt - zf) * mun_ref[...]
    w = RWKV_WIDTH
    r = zs[:, Z_R:Z_R + w]
    k = zs[:, Z_K:Z_K + w]
    v = zs[:, Z_V:Z_V + w]
    wd = zs[:, Z_WD:Z_WD + 2 * DECAY_LORA]
    ad = zs[:, Z_AD:Z_AD + 2 * ICLR_LORA]
    gd = zs[:, Z_GD:Z_GD + GATE_PAD]
    lw = -math.exp(-0.5) * _sigmoid(w0_ref[...] + _dot(jnp.tanh(wd).astype(BF), w2_ref[...]))
    a = _sigmoid(a0_ref[...] + _dot(ad.astype(BF), a2_ref[...]))
    g = _dot(_sigmoid(gd).astype(BF), g2_ref[...])
    kq = k * kk_ref[...]
    ssq = _seg_sum(kq * kq, _seg_ones(RWKV_HD))
    kk = kq * lax.rsqrt(jnp.maximum(ssq, 1e-24))
    r_o[...] = r.astype(BF)
    k_o[...] = k.astype(BF)
    v_o[...] = v.astype(BF)
    kk_o[...] = kk.astype(BF)
    a0_o[...] = a[:, :w].astype(BF)
    a1_o[...] = a[:, w:].astype(BF)
    lw0_o[...] = lw[:, :w]
    lw1_o[...] = lw[:, w:]
    g_o[...] = g.astype(BF)


def rwkv_prep(z, p, dims):
    rows = z.shape[0]
    tt = _row_tile(256, dims, within_seq=True)
    hb = SUBLANES_BF16
    nhb = rows // hb
    w = RWKV_WIDTH
    full = lambda shape: pl.BlockSpec(shape, lambda i: (0,) * len(shape))
    out_bf = jax.ShapeDtypeStruct((rows, w), BF)
    out_f = jax.ShapeDtypeStruct((rows, w), F32)
    ospec = pl.BlockSpec((tt, w), lambda i: (i, 0))
    return pl.pallas_call(
        functools.partial(_rwkv_prep_kernel, dims=dims),
        grid=(rows // tt,),
        in_specs=[pl.BlockSpec((tt, RWKV_PAD), lambda i: (i, 0)),
                  pl.BlockSpec((hb, RWKV_PAD), lambda i: (jnp.maximum(i * (tt // hb) - 1, 0), 0)),
                  pl.BlockSpec((hb, RWKV_PAD), lambda i: (jnp.minimum((i + 1) * (tt // hb), nhb - 1), 0)),
                  full((1, RWKV_PAD)), full((1, RWKV_PAD)),
                  full((1, 2 * w)), full((2 * DECAY_LORA, 2 * w)),
                  full((1, 2 * w)), full((2 * ICLR_LORA, 2 * w)),
                  full((GATE_PAD, w)), full((1, w))],
        out_specs=[ospec] * 9,
        out_shape=[out_bf, out_bf, out_bf, out_bf, out_bf, out_bf, out_f, out_f, out_bf],
        compiler_params=_cparams(1),
        name="rwkv_prep",
    )(z, z, z, p['mu_prev'], p['mu_next'], p['w0'], p['w2'], p['a0'], p['a2'], p['g2'], p['k_k'])


def _tri_inverse(lb, same_sub):
    c = lb.shape[0]
    eye = jnp.where(lax.broadcasted_iota(jnp.int32, (c, c), 0) == lax.broadcasted_iota(jnp.int32, (c, c), 1),
                    1.0, 0.0)
    d1 = jnp.where(same_sub, lb, 0.0)
    e = lb - d1
    td = eye - d1
    dp = d1
    for _ in range(int(math.log2(SCAN_SUB)) - 1):
        dp = _dot_hp(dp, dp)
        td = _dot_hp(td, eye + dp)
    f = _dot_hp(td, e)
    g = eye - f
    fp = f
    for _ in range(int(math.log2(c // SCAN_SUB)) - 1):
        fp = _dot_hp(fp, fp)
        g = _dot_hp(g, eye + fp)
    return _dot_hp(g, td)


def _scan_direction(r, k, v, kk, a, lw, ka, s_ref, y_ref, reverse):
    c, wdt = r.shape
    row = lax.broadcasted_iota(jnp.int32, (c, c), 0)
    col = lax.broadcasted_iota(jnp.int32, (c, c), 1)
    before = (col > row) if reverse else (col < row)
    upto = (col >= row) if reverse else (col <= row)
    same_sub = (row // SCAN_SUB) == (col // SCAN_SUB)
    kmod = k * (1.0 + (a - 1.0) * ka)
    b = kk * a
    cum = _dot_exact_lhs(jnp.where(upto, 1.0, 0.0).astype(BF), lw)
    tot = jnp.sum(lw, axis=0, keepdims=True)
    g_in, g_ex, g_inv, g_end = jnp.exp(cum), jnp.exp(cum - lw), jnp.exp(-cum), jnp.exp(tot - cum)
    g_tot = jnp.exp(tot)
    kap_t = (kk * g_ex).astype(BF)
    r_t = (r * g_in).astype(BF)
    km_t = (kmod * g_inv).astype(BF)
    b_t = (b * g_inv).astype(BF)
    km_e = (kmod * g_end).astype(BF)
    b_e = (b * g_end).astype(BF)
    vb = v.astype(BF)
    for h in range(wdt // RWKV_HD):
        sl = slice(h * RWKV_HD, (h + 1) * RWKV_HD)
        s0 = s_ref[h]
        s0b = s0.astype(BF)
        lk = jnp.where(before, _dot_nt(kap_t[:, sl], km_t[:, sl]), 0.0)
        lb = jnp.where(before, _dot_nt(kap_t[:, sl], b_t[:, sl]), 0.0)
        mk = jnp.where(upto, _dot_nt(r_t[:, sl], km_t[:, sl]), 0.0)
        mb = jnp.where(upto, _dot_nt(r_t[:, sl], b_t[:, sl]), 0.0)
        x = _dot_nt(kap_t[:, sl], s0b) + _dot(lk.astype(BF), vb[:, sl])
        u = _dot_hp(_tri_inverse(lb, same_sub), x)
        ub = u.astype(BF)
        y = _dot_nt(r_t[:, sl], s0b) + _dot(mk.astype(BF), vb[:, sl]) - _dot(mb.astype(BF), ub)
        y_ref[:, sl] = y
        s_ref[h] = s0 * g_tot[:, sl] + _dot_tn(vb[:, sl], km_e[:, sl]) - _dot_tn(ub, b_e[:, sl])


def _rwkv_scan_kernel(rf, kf, vf, kkf, af, lwf, rr, kr, vr, kkr, ar, lwr, ka_ref, y0_ref, y1_ref, s_ref):
    @pl.when(pl.program_id(2) == 0)
    def _():
        s_ref[...] = jnp.zeros(s_ref.shape, F32)

    ka = ka_ref[...]
    ld = lambda ref: ref[...].astype(F32)
    _scan_direction(ld(rf), ld(kf), ld(vf), ld(kkf), ld(af), lwf[...], ka, s_ref.at[0], y0_ref, False)
    _scan_direction(ld(rr), ld(kr), ld(vr), ld(kkr), ld(ar), lwr[...], ka, s_ref.at[1], y1_ref, True)


def rwkv_scan(r, k, v, kk, a0, a1, lw0, lw1, k_a, dims):
    c = SCAN_CHUNK
    ncc, nlc = dims.cl // c, dims.s // c
    lat0 = dims.r_ctx // c
    wdt = SCAN_HEADS * RWKV_HD

    def fwd(b, hb, t):
        return (jnp.where(t < ncc, b * ncc + t, lat0 + b * nlc + (t - ncc)), hb)

    def rev(b, hb, t):
        return (jnp.where(t < ncc, b * ncc + (ncc - 1 - t), lat0 + b * nlc + (nlc - 1 - (t - ncc))), hb)

    fs = pl.BlockSpec((c, wdt), fwd)
    rs = pl.BlockSpec((c, wdt), rev)
    y_shape = jax.ShapeDtypeStruct(lw0.shape, F32)
    return pl.pallas_call(
        _rwkv_scan_kernel,
        grid=(dims.nb, RWKV_WIDTH // wdt, ncc + nlc),
        in_specs=[fs] * 6 + [rs] * 6 + [pl.BlockSpec((1, wdt), lambda b, hb, t: (0, hb))],
        out_specs=[fs, rs],
        out_shape=[y_shape, y_shape],
        scratch_shapes=[pltpu.VMEM((2, SCAN_HEADS, RWKV_HD, RWKV_HD), F32)],
        compiler_params=_cparams(3),
        name="rwkv_scan",
    )(r, k, v, kk, a0, lw0, r, k, v, kk, a1, lw1, k_a)


def _rwkv_out_kernel(y0_ref, y1_ref, r_ref, k_ref, v_ref, a0_ref, a1_ref, g_ref, ka_ref, rk_ref, lng_ref, lnb_ref,
                     o_ref):
    seg = _seg_ones(RWKV_HD)
    y = y0_ref[...] + y1_ref[...]
    mu = _seg_sum(y, seg) * (1.0 / RWKV_HD)
    yc = y - mu
    var = _seg_sum(yc * yc, seg) * (1.0 / RWKV_HD)
    yn = yc * lax.rsqrt(var + RWKV_GN_EPS) * lng_ref[...] + lnb_ref[...]
    ld = lambda ref: ref[...].astype(F32)
    r, k, v = ld(r_ref), ld(k_ref), ld(v_ref)
    kmod_sum = k * (2.0 + (ld(a0_ref) + ld(a1_ref) - 2.0) * ka_ref[...])
    bonus = _seg_sum(r * kmod_sum * rk_ref[...], seg) * v
    o_ref[...] = ((yn + bonus) * ld(g_ref)).astype(BF)


def rwkv_out(y0, y1, r, k, v, a0, a1, g, p, dims, row0=0):
    rows, w = y0.shape
    tr = _row_tile(256, dims)
    b0 = row0 // tr
    rs = pl.BlockSpec((tr, w), lambda i: (i + b0, 0))
    ps = pl.BlockSpec((1, w), lambda i: (0, 0))
    return pl.pallas_call(
        _rwkv_out_kernel,
        grid=((rows - row0) // tr,),
        in_specs=[rs] * 8 + [ps] * 4,
        out_specs=pl.BlockSpec((tr, w), lambda i: (i, 0)),
        out_shape=jax.ShapeDtypeStruct((rows - row0, w), BF),
        compiler_params=_cparams(1),
        name="rwkv_out",
    )(y0, y1, r, k, v, a0, a1, g, p['k_a'], p['r_k'], p['ln_g'], p['ln_b'])


def _conv_kernel(val_ref, gate_ref, vp_ref, gp_ref, vn_ref, gn_ref, w_ref, b_ref, lng_ref, lnb_ref, o_ref,
                 u_ref, acc_ref, *, dims, b0):
    i = pl.program_id(0) + b0
    tt = val_ref.shape[0]
    hb = vp_ref.shape[0]
    pos, cnt = _seq_pos(i, tt, dims)
    glu = lambda a, b: a.astype(F32) * _sigmoid(b.astype(F32))
    u_ref[0:hb, :] = jnp.where(pos > 0, glu(vp_ref[...], gp_ref[...]), 0.0)
    u_ref[hb:hb + tt, :] = glu(val_ref[...], gate_ref[...])
    u_ref[hb + tt:2 * hb + tt, :] = jnp.where(pos < cnt - 1, glu(vn_ref[...], gn_ref[...]), 0.0)
    half = CONV_K // 2
    for c0 in range(0, CONV_CH, LANES):
        acc = jnp.zeros((tt, LANES), F32) + b_ref[:, c0:c0 + LANES]
        for j in range(CONV_K):
            off = hb - half + j
            acc = acc + u_ref[off:off + tt, c0:c0 + LANES] * w_ref[j:j + 1, c0:c0 + LANES]
        acc_ref[:, c0:c0 + LANES] = acc
    y = acc_ref[...]
    mu = jnp.mean(y, axis=-1, keepdims=True)
    yc = y - mu
    var = jnp.mean(yc * yc, axis=-1, keepdims=True)
    yn = yc * lax.rsqrt(var + 1e-5) * lng_ref[...] + lnb_ref[...]
    o_ref[...] = (yn * _sigmoid(yn)).astype(BF)


def conformer_conv(z, p, dims, row0=0):
    rows = z.shape[0]
    tt = _row_tile(256, dims, within_seq=True)
    hb = SUBLANES_BF16
    nhb = rows // hb
    b0 = row0 // tt
    vcol, gcol = Z_CONV // CONV_CH, Z_CONV // CONV_CH + 1
    prev = lambda i: jnp.maximum((i + b0) * (tt // hb) - 1, 0)
    nxt = lambda i: jnp.minimum((i + b0 + 1) * (tt // hb), nhb - 1)
    ps = pl.BlockSpec((1, CONV_CH), lambda i: (0, 0))
    return pl.pallas_call(
        functools.partial(_conv_kernel, dims=dims, b0=b0),
        grid=((rows - row0) // tt,),
        in_specs=[pl.BlockSpec((tt, CONV_CH), lambda i: (i + b0, vcol)),
                  pl.BlockSpec((tt, CONV_CH), lambda i: (i + b0, gcol)),
                  pl.BlockSpec((hb, CONV_CH), lambda i: (prev(i), vcol)),
                  pl.BlockSpec((hb, CONV_CH), lambda i: (prev(i), gcol)),
                  pl.BlockSpec((hb, CONV_CH), lambda i: (nxt(i), vcol)),
                  pl.BlockSpec((hb, CONV_CH), lambda i: (nxt(i), gcol)),
                  pl.BlockSpec((CONV_K + 1, CONV_CH), lambda i: (0, 0)),
                  ps, ps, ps],
        out_specs=pl.BlockSpec((tt, CONV_CH), lambda i: (i, 0)),
        out_shape=jax.ShapeDtypeStruct((rows - row0, CONV_CH), BF),
        scratch_shapes=[pltpu.VMEM((tt + 2 * hb, CONV_CH), F32), pltpu.VMEM((tt, CONV_CH), F32)],
        compiler_params=_cparams(1),
        name="conformer_conv",
    )(z, z, z, z, z, z, p['conv_w'], p['conv_b'], p['conv_ln_g'], p['conv_ln_b'])


def _out_proj_kernel(*refs, n_ctx_blocks):
    if n_ctx_blocks:
        ac_ref, al_ref, r_ref, c_ref, wa_ref, wr_ref, wc_ref, x_ref, g_ref, o_ref = refs
        a = jnp.where(pl.program_id(0) < n_ctx_blocks, ac_ref[...], al_ref[...])
    else:
        al_ref, r_ref, c_ref, wa_ref, wr_ref, wc_ref, x_ref, g_ref, o_ref = refs
        a = al_ref[...]
    acc = _dot(a, wa_ref[...]) + _dot(r_ref[...], wr_ref[...]) + _dot(c_ref[...], wc_ref[...])
    o_ref[...] = x_ref[...] + g_ref[0] * acc


def out_proj(a_ctx, a_lat, rw, cv, w_out, x, gate, dims, row0=0):
    rows, d = x.shape
    tm = _row_tile(1024, dims)
    tn = 512
    b0 = row0 // tm
    nctx = dims.r_ctx // tm
    n_ctx_blocks = nctx - b0
    assert (a_ctx is not None) == (n_ctx_blocks > 0)
    a_specs, a_args = [], []
    if n_ctx_blocks:
        a_specs.append(pl.BlockSpec((tm, Q_COLS), lambda i, j: (jnp.minimum(i, nctx - 1), 0)))
        a_args.append(a_ctx)
    a_specs.append(pl.BlockSpec((tm, Q_COLS), lambda i, j: (jnp.maximum(i + b0 - nctx, 0), 0)))
    a_args.append(a_lat)
    return pl.pallas_call(
        functools.partial(_out_proj_kernel, n_ctx_blocks=n_ctx_blocks),
        grid=((rows - row0) // tm, d // tn),
        in_specs=a_specs + [
                  pl.BlockSpec((tm, RWKV_WIDTH), lambda i, j: (i, 0)),
                  pl.BlockSpec((tm, CONV_CH), lambda i, j: (i, 0)),
                  pl.BlockSpec((Q_COLS, tn), lambda i, j: (0, j)),
                  pl.BlockSpec((RWKV_WIDTH, tn), lambda i, j: (0, j)),
                  pl.BlockSpec((CONV_CH, tn), lambda i, j: (0, j)),
                  pl.BlockSpec((tm, tn), lambda i, j: (i + b0, j)),
                  pl.BlockSpec((1, 1, tn), lambda i, j: (_grp(i + b0, tm, dims), 0, j))],
        out_specs=pl.BlockSpec((tm, tn), lambda i, j: (i + b0, j)),
        out_shape=jax.ShapeDtypeStruct(x.shape, F32),
        input_output_aliases={len(a_args) + 5: 0},
        compiler_params=_cparams(2),
        name="out_proj",
    )(*a_args, rw, cv, w_out[:Q_COLS], w_out[Q_COLS:Q_COLS + RWKV_WIDTH], w_out[Q_COLS + RWKV_WIDTH:], x, gate)


def _row_gather(idx_ref, n, src_hbm, dst, sem):
    def body(r, carry):
        t = idx_ref[0, 0, r]
        pltpu.make_async_copy(src_hbm.at[pl.ds(t, 1)], dst.at[pl.ds(r, 1)], sem).start()
        return carry
    lax.fori_loop(0, n, body, 0)


def _moe_kernel(be_ref, act_ref, tok_ref, tokn_ref, wt_ref, h_hbm, wg_ref, wu_ref, wd_ref, o_ref, xbuf, sem):
    del be_ref
    i = pl.program_id(0)
    nblk = pl.num_programs(0)
    slot = i % 2
    rows = xbuf.shape[1]

    @pl.when(i == 0)
    def _():
        _row_gather(tok_ref, rows, h_hbm, xbuf.at[0], sem.at[0])

    @pl.when(i + 1 < nblk)
    def _():
        _row_gather(tokn_ref, rows, h_hbm, xbuf.at[1 - slot], sem.at[1 - slot])

    pltpu.make_async_copy(h_hbm.at[pl.ds(0, rows)], xbuf.at[slot], sem.at[slot]).wait()

    @pl.when(act_ref[i] > 0)
    def _():
        xb = xbuf[slot].astype(BF)
        gate = _dot(xb, wg_ref[0])
        up = _dot(xb, wu_ref[0])
        hid = (gate * _sigmoid(gate) * up).astype(BF)
        o_ref[...] = _dot(hid, wd_ref[0]) * wt_ref[0]

    @pl.when(act_ref[i] == 0)
    def _():
        o_ref[...] = jnp.zeros(o_ref.shape, F32)


def moe_experts(h2, blk_expert, blk_active, slot_tok, slot_wt, w_gate, w_up, w_down):
    n_blk = blk_expert.shape[0]
    d = h2.shape[1]
    de = w_gate.shape[2]
    b = MOE_ROWS
    tok3 = slot_tok.reshape(n_blk, 1, b)
    grid_spec = pltpu.PrefetchScalarGridSpec(
        num_scalar_prefetch=2,
        grid=(n_blk,),
        in_specs=[pl.BlockSpec((1, 1, b), lambda i, be, act: (i, 0, 0), memory_space=pltpu.SMEM),
                  pl.BlockSpec((1, 1, b), lambda i, be, act: (jnp.minimum(i + 1, n_blk - 1), 0, 0),
                               memory_space=pltpu.SMEM),
                  pl.BlockSpec((1, b, 1), lambda i, be, act: (i, 0, 0)),
                  pl.BlockSpec(memory_space=pl.ANY),
                  pl.BlockSpec((1, d, de), lambda i, be, act: (be[i], 0, 0)),
                  pl.BlockSpec((1, d, de), lambda i, be, act: (be[i], 0, 0)),
                  pl.BlockSpec((1, de, d), lambda i, be, act: (be[i], 0, 0))],
        out_specs=pl.BlockSpec((b, d), lambda i, be, act: (i, 0)),
        scratch_shapes=[pltpu.VMEM((2, b, d), F32), pltpu.SemaphoreType.DMA((2,))],
    )
    return pl.pallas_call(
        _moe_kernel,
        grid_spec=grid_spec,
        out_shape=jax.ShapeDtypeStruct((n_blk * b, d), F32),
        compiler_params=_cparams(1),
        name="moe_experts",
    )(blk_expert, blk_active, tok3, tok3, slot_wt.reshape(n_blk, b, 1), h2, w_gate, w_up, w_down)


def _combine_kernel(pos_ref, posn_ref, o_hbm, x_ref, g_ref, out_ref, buf, sem):
    i = pl.program_id(0)
    nblk = pl.num_programs(0)
    slot = i % 2
    n2 = buf.shape[1]

    @pl.when(i == 0)
    def _():
        _row_gather(pos_ref, n2, o_hbm, buf.at[0], sem.at[0])

    @pl.when(i + 1 < nblk)
    def _():
        _row_gather(posn_ref, n2, o_hbm, buf.at[1 - slot], sem.at[1 - slot])

    pltpu.make_async_copy(o_hbm.at[pl.ds(0, n2)], buf.at[slot], sem.at[slot]).wait()
    tb = n2 // TOP_K
    y = buf[slot, 0:tb, :] + buf[slot, tb:n2, :]
    out_ref[...] = x_ref[...] + g_ref[0] * y


def moe_combine(o_slots, pos, x, gate, dims, row0=0):
    rows, d = x.shape
    tb = COMBINE_ROWS
    n_blk = (rows - row0) // tb
    b0 = row0 // tb
    posb = pos.reshape(n_blk, tb, TOP_K).transpose(0, 2, 1).reshape(n_blk, 1, TOP_K * tb)
    return pl.pallas_call(
        _combine_kernel,
        grid=(n_blk,),
        in_specs=[pl.BlockSpec((1, 1, TOP_K * tb), lambda i: (i, 0, 0), memory_space=pltpu.SMEM),
                  pl.BlockSpec((1, 1, TOP_K * tb), lambda i: (jnp.minimum(i + 1, n_blk - 1), 0, 0),
                               memory_space=pltpu.SMEM),
                  pl.BlockSpec(memory_space=pl.ANY),
                  pl.BlockSpec((tb, d), lambda i: (i + b0, 0)),
                  pl.BlockSpec((1, 1, d), lambda i: (_grp(i + b0, tb, dims), 0, 0))],
        out_specs=pl.BlockSpec((tb, d), lambda i: (i + b0, 0)),
        out_shape=jax.ShapeDtypeStruct(x.shape, F32),
        scratch_shapes=[pltpu.VMEM((2, TOP_K * tb, d), F32), pltpu.SemaphoreType.DMA((2,))],
        input_output_aliases={3: 0},
        compiler_params=_cparams(1),
        name="moe_combine",
    )(posb, posb, o_slots, x, gate)


def moe_routing(logits, router_b):
    n = logits.shape[0]
    scores = jax.nn.sigmoid(logits[:, :N_EXPERTS])
    sel = (scores + router_b.astype(F32)).reshape(n, N_GROUPS, EXPERTS_PER_GROUP)
    grp_score = jnp.sum(lax.top_k(sel, TOP_K)[0], axis=-1)
    g_idx = jnp.argmax(grp_score, axis=-1)
    in_grp = jnp.take_along_axis(sel, g_idx[:, None, None], axis=1)[:, 0]
    _, local = lax.top_k(in_grp, TOP_K)
    expert = g_idx[:, None] * EXPERTS_PER_GROUP + local
    wts = jnp.take_along_axis(scores, expert, axis=1)
    wts = wts / jnp.sum(wts, axis=-1, keepdims=True)

    b = MOE_ROWS
    nk = n * TOP_K
    e_flat = expert.reshape(-1).astype(jnp.int32)
    tok = jnp.repeat(jnp.arange(n, dtype=jnp.int32), TOP_K)
    order = jnp.argsort(e_flat)
    e_s, tok_s, wt_s = e_flat[order], tok[order], wts.reshape(-1)[order]
    counts = jnp.bincount(e_flat, length=N_EXPERTS)
    start = jnp.cumsum(counts) - counts
    padded = (counts + b - 1) // b * b
    pend = jnp.cumsum(padded)
    pstart = pend - padded
    dest = (pstart[e_s] + jnp.arange(nk) - start[e_s]).astype(jnp.int32)
    n_blocks = -(-(nk + N_EXPERTS * (b - 1)) // b)
    slots = n_blocks * b
    slot_tok = jnp.zeros((slots,), jnp.int32).at[dest].set(tok_s)
    slot_wt = jnp.zeros((slots,), F32).at[dest].set(wt_s)
    pos = jnp.zeros((nk,), jnp.int32).at[order].set(dest).reshape(n, TOP_K)
    blk_start = jnp.arange(n_blocks) * b
    blk_expert = jnp.clip(jnp.searchsorted(pend, blk_start, side='right'), 0, N_EXPERTS - 1).astype(jnp.int32)
    blk_active = (blk_start < pend[-1]).astype(jnp.int32)
    return blk_expert, blk_active, slot_tok, slot_wt, pos


def _rope_tables(dims):
    rows = dims.s // GRID_W
    row = jnp.repeat(jnp.arange(rows), GRID_W).astype(F32)
    col = jnp.tile(jnp.arange(GRID_W), rows).astype(F32)
    inv = ROPE_BASE ** (-jnp.arange(ROPE_PAIRS, dtype=F32) / ROPE_PAIRS)
    ang = jnp.concatenate([row[:, None] * inv, row[:, None] * inv, col[:, None] * inv, col[:, None] * inv], axis=1)
    sign = jnp.tile(jnp.concatenate([-jnp.ones((ROPE_PAIRS,), F32), jnp.ones((ROPE_PAIRS,), F32)]), 2)
    cos_t = jnp.concatenate([jnp.ones((dims.cl, ATT_HD), F32), jnp.cos(ang)], axis=0)
    sin_t = jnp.concatenate([jnp.zeros((dims.cl, ATT_HD), F32), jnp.sin(ang) * sign], axis=0)
    return cos_t, sin_t


def _block_diag2(w):
    z = jnp.zeros_like(w[0])
    return jnp.concatenate([jnp.concatenate([w[0], z], axis=1), jnp.concatenate([z, w[1]], axis=1)], axis=0)


def _pad_cols(v, n):
    return jnp.pad(v, [(0, 0)] * (v.ndim - 1) + [(0, n - v.shape[-1])])


def _layer_params(l, w_in, q_norm_g, k_norm_g, attn_sink, rwkv_mu_prev, rwkv_mu_next, rwkv_w0, rwkv_w2, rwkv_a0,
                  rwkv_a2, rwkv_g2, rwkv_k_k, rwkv_k_a, rwkv_r_k, rwkv_ln_g, rwkv_ln_b, conv_w, conv_b, conv_ln_g,
                  conv_ln_b, dims):
    d = w_in.shape[1]
    wl = w_in[l]
    w_in_p = jnp.concatenate([wl[:, ATT_COLS:ATT_COLS + RWKV_COLS], jnp.zeros((d, RWKV_PAD - RWKV_COLS), F32),
                              wl[:, ATT_COLS + RWKV_COLS:], wl[:, :ATT_COLS]], axis=1).astype(BF)
    scale = ATT_HD ** -0.5
    gains = jnp.concatenate([jnp.tile(q_norm_g[l] * scale, (ATT_HEADS, 1)), jnp.tile(k_norm_g[l], (KV_HEADS, 1))],
                            axis=0).reshape(ATT_HEADS + KV_HEADS, 1, ATT_HD)
    sink = attn_sink[l].astype(F32).reshape(KV_HEADS, KV_GROUP, 1)
    return {
        'w_in': w_in_p,
        'gains': gains,
        'sink_lat': jnp.repeat(sink, ATT_BLOCK, axis=1).reshape(KV_HEADS, KV_GROUP * ATT_BLOCK, 1),
        'sink_ctx': jnp.repeat(sink, dims.cl, axis=1).reshape(KV_HEADS, KV_GROUP * dims.cl, 1),
        'mu_prev': _pad_cols(rwkv_mu_prev[l][None], RWKV_PAD),
        'mu_next': _pad_cols(rwkv_mu_next[l][None], RWKV_PAD),
        'w0': rwkv_w0[l].reshape(1, 2 * RWKV_WIDTH),
        'w2': _block_diag2(rwkv_w2[l]).astype(BF),
        'a0': rwkv_a0[l].reshape(1, 2 * RWKV_WIDTH),
        'a2': _block_diag2(rwkv_a2[l]).astype(BF),
        'g2': jnp.pad(rwkv_g2[l], ((0, GATE_PAD - GATE_LORA), (0, 0))).astype(BF),
        'k_k': rwkv_k_k[l][None],
        'k_a': rwkv_k_a[l][None],
        'r_k': rwkv_r_k[l].reshape(1, RWKV_WIDTH),
        'ln_g': rwkv_ln_g[l][None],
        'ln_b': rwkv_ln_b[l][None],
        'conv_w': jnp.pad(conv_w[l], ((0, 1), (0, 0))),
        'conv_b': conv_b[l][None],
        'conv_ln_g': conv_ln_g[l][None],
        'conv_ln_b': conv_ln_b[l][None],
    }


def kernel(x, c, ctx, c_ctx, ada_w, ada_b, norm1_g, norm2_g, w_in, q_norm_g, k_norm_g, attn_sink, rwkv_mu_prev, rwkv_mu_next, rwkv_w0, rwkv_w2, rwkv_a0, rwkv_a2, rwkv_g2, rwkv_k_k, rwkv_k_a, rwkv_r_k, rwkv_ln_g, rwkv_ln_b, conv_w, conv_b, conv_ln_g, conv_ln_b, w_out, router_w, router_b, moe_w_gate, moe_w_up, moe_w_down):
    nb, s, d = x.shape
    dims = Dims(nb=nb, s=s, cl=ctx.shape[1])
    n_layers = w_in.shape[0]
    mod_rows = -(-(nb + 1) // 8) * 8
    cs = jnp.zeros((mod_rows, d), F32).at[:nb].set(c).at[nb].set(c_ctx)
    mods = ada_tables(cs, ada_w, ada_b).reshape(n_layers, mod_rows, 6, 1, d)
    cos_t, sin_t = _rope_tables(dims)
    router_p = _pad_cols(router_w.astype(F32), LANES)
    xa = jnp.concatenate([ctx.reshape(dims.r_ctx, d), x.reshape(dims.r_lat, d)], axis=0)

    for l in range(n_layers):
        ctx_out = l < n_layers - 1
        row0 = 0 if ctx_out else dims.r_ctx
        p = _layer_params(l, w_in, q_norm_g, k_norm_g, attn_sink, rwkv_mu_prev, rwkv_mu_next, rwkv_w0, rwkv_w2,
                          rwkv_a0, rwkv_a2, rwkv_g2, rwkv_k_k, rwkv_k_a, rwkv_r_k, rwkv_ln_g, rwkv_ln_b, conv_w,
                          conv_b, conv_ln_g, conv_ln_b, dims)
        sh1, sc1, g1, sh2, sc2, g2 = [mods[l, :, m] for m in range(6)]

        h1 = norm_mod(xa, norm1_g[l], sh1, sc1, dims)
        z = in_proj(h1, p['w_in'], dims)

        qk = qk_prep(z, p['gains'], cos_t, sin_t, dims)
        att_lat = attn_latent(qk, z, p['sink_lat'], dims)
        att_ctx = attn_context(qk, z, p['sink_ctx'], dims) if ctx_out else None

        r, k, v, kk, a0, a1, lw0, lw1, g = rwkv_prep(z, p, dims)
        y0, y1 = rwkv_scan(r, k, v, kk, a0, a1, lw0, lw1, p['k_a'], dims)
        rw = rwkv_out(y0, y1, r, k, v, a0, a1, g, p, dims, row0=row0)

        cv = conformer_conv(z, p, dims, row0=row0)
        xa = out_proj(att_ctx, att_lat, rw, cv, w_out[l].astype(BF), xa, g1, dims, row0=row0)

        h2, logits = norm_mod(xa, norm2_g[l], sh2, sc2, dims, row0=row0, router_w=router_p)
        blk_expert, blk_active, slot_tok, slot_wt, pos = moe_routing(logits, router_b)
        o_slots = moe_experts(h2, blk_expert, blk_active, slot_tok, slot_wt, moe_w_gate[l].astype(BF),
                              moe_w_up[l].astype(BF), moe_w_down[l].astype(BF))
        xa = moe_combine(o_slots, pos, xa, g2, dims, row0=row0)

    return xa[dims.r_ctx:].reshape(nb, s, d)
```

```python
import functools
import math
from typing import NamedTuple

import jax
import jax.numpy as jnp
from jax import lax
from jax.experimental import pallas as pl
from jax.experimental.pallas import tpu as pltpu

F32 = jnp.float32
BF = jnp.bfloat16

D_MODEL = 4096
DEPTH = 2
GRID_W = 64
ATT_HD = 128
ATT_HEADS = 12
KV_HEADS = 4
KV_GROUP = ATT_HEADS // KV_HEADS
ATT_BLOCK = 128
ROPE_BASE = 10000.0
ROPE_PAIRS = ATT_HD // 4
NEG_INF = -1e30
RWKV_HD = 64
RWKV_HEADS = 24
RWKV_WIDTH = RWKV_HEADS * RWKV_HD
DECAY_LORA = 64
ICLR_LORA = 64
GATE_LORA = 224
RWKV_GN_EPS = 64e-5
CONV_CH = 1024
CONV_K = 31
Q_COLS = ATT_HEADS * ATT_HD
KV_COLS = KV_HEADS * ATT_HD
ATT_COLS = Q_COLS + 2 * KV_COLS
RWKV_COLS = 3 * RWKV_WIDTH + 2 * DECAY_LORA + 2 * ICLR_LORA + GATE_LORA
CONV_COLS = 2 * CONV_CH
N_EXPERTS = 32
N_GROUPS = 4
EXPERTS_PER_GROUP = N_EXPERTS // N_GROUPS
TOP_K = 2
D_EXPERT = 640

LANES = 128
SUBLANES_BF16 = 16
VMEM_LIMIT = 56 * 1024 * 1024

RWKV_PAD = 5120
GATE_PAD = RWKV_PAD - (3 * RWKV_WIDTH + 2 * DECAY_LORA + 2 * ICLR_LORA)
Z_R, Z_K, Z_V = 0, RWKV_WIDTH, 2 * RWKV_WIDTH
Z_WD = 3 * RWKV_WIDTH
Z_AD = Z_WD + 2 * DECAY_LORA
Z_GD = Z_AD + 2 * ICLR_LORA
Z_CONV = RWKV_PAD
Z_ATT = RWKV_PAD + CONV_COLS
Z_COLS = Z_ATT + ATT_COLS

SCAN_CHUNK = 64
SCAN_GROUP_HEADS = 2
SCAN_GROUPS = 4
MOE_ROWS = 256
COMBINE_ROWS = 128


class Dims(NamedTuple):
    nb: int
    s: int
    cl: int

    @property
    def r_ctx(self):
        return self.nb * self.cl

    @property
    def r_lat(self):
        return self.nb * self.s

    @property
    def rows(self):
        return self.r_ctx + self.r_lat


def _cparams(n_axes):
    return pltpu.CompilerParams(dimension_semantics=("arbitrary",) * n_axes, vmem_limit_bytes=VMEM_LIMIT)


def _row_tile(pref, dims, within_seq=False):
    t = math.gcd(pref, math.gcd(dims.r_ctx, dims.s))
    if within_seq:
        t = math.gcd(t, dims.cl)
    return t


def _grp(i, tr, dims):
    nctx = dims.r_ctx // tr
    per = dims.s // tr
    return jnp.where(i < nctx, dims.nb, (i - nctx) // per)


def _seq_pos(i, tt, dims):
    nctx = dims.r_ctx // tt
    pc, pl_ = dims.cl // tt, dims.s // tt
    is_ctx = i < nctx
    pos = jnp.where(is_ctx, i % pc, (i - nctx) % pl_)
    cnt = jnp.where(is_ctx, pc, pl_)
    return pos, cnt


def _sigmoid(x):
    return 1.0 / (1.0 + jnp.exp(-x))


def _dot(a, b):
    return jnp.dot(a, b, preferred_element_type=F32)


def _dot_nt(a, b):
    return lax.dot_general(a, b, (((1,), (1,)), ((), ())), preferred_element_type=F32)


def _dot_tn(a, b):
    return lax.dot_general(a, b, (((0,), (0,)), ((), ())), preferred_element_type=F32)


def _split3(x):
    hi = x.astype(BF)
    r1 = x - hi.astype(F32)
    mid = r1.astype(BF)
    lo = (r1 - mid.astype(F32)).astype(BF)
    return hi, mid, lo


def _dot_exact_lhs(m_bf, x):
    hi, mid, lo = _split3(x)
    return _dot(m_bf, hi) + _dot(m_bf, mid) + _dot(m_bf, lo)


def _dot_exact_rhs(x, m_bf):
    hi, mid, lo = _split3(x)
    return _dot(hi, m_bf) + _dot(mid, m_bf) + _dot(lo, m_bf)


def _dot_hp(a, b):
    ah = a.astype(BF)
    al = (a - ah.astype(F32)).astype(BF)
    bh = b.astype(BF)
    bl = (b - bh.astype(F32)).astype(BF)
    return _dot(ah, bh) + (_dot(ah, bl) + _dot(al, bh))


def _seg_ones(width):
    i = lax.broadcasted_iota(jnp.int32, (LANES, LANES), 0) // width
    j = lax.broadcasted_iota(jnp.int32, (LANES, LANES), 1) // width
    return jnp.where(i == j, 1.0, 0.0).astype(BF)


def _seg_sum(x, seg):
    parts = [_dot_exact_rhs(x[:, j:j + LANES], seg) for j in range(0, x.shape[1], LANES)]
    return parts[0] if len(parts) == 1 else jnp.concatenate(parts, axis=1)


def _ada_kernel(c_ref, w_ref, b_ref, o_ref):
    cv = c_ref[...]
    act = (cv * _sigmoid(cv)).astype(BF)
    o_ref[0] = _dot(act, w_ref[0].astype(BF)) + b_ref[0]


def ada_tables(cs, ada_w, ada_b):
    n_layers, d, n = ada_w.shape
    tn = 512
    return pl.pallas_call(
        _ada_kernel,
        grid=(n_layers, n // tn),
        in_specs=[pl.BlockSpec((cs.shape[0], d), lambda l, j: (0, 0)),
                  pl.BlockSpec((1, d, tn), lambda l, j: (l, 0, j)),
                  pl.BlockSpec((1, 1, tn), lambda l, j: (l, 0, j))],
        out_specs=pl.BlockSpec((1, cs.shape[0], tn), lambda l, j: (l, 0, j)),
        out_shape=jax.ShapeDtypeStruct((n_layers, cs.shape[0], n), F32),
        compiler_params=_cparams(2),
        name="ada_tables",
    )(cs, ada_w, ada_b.reshape(n_layers, 1, n))


def _norm_mod_kernel(x_ref, g_ref, sh_ref, sc_ref, *rest, with_router):
    x = x_ref[...]
    ms = jnp.mean(x * x, axis=-1, keepdims=True)
    y = x * lax.rsqrt(ms + 1e-6) * g_ref[...]
    h = y * (1.0 + sc_ref[0]) + sh_ref[0]
    if with_router:
        rw_ref, h_ref, lg_ref = rest
        h_ref[...] = h
        lg_ref[...] = jnp.dot(h, rw_ref[...], precision=lax.Precision.HIGHEST, preferred_element_type=F32)
    else:
        (h_ref,) = rest
        h_ref[...] = h.astype(BF)


def norm_mod(x, g, sh, sc, dims, row0=0, router_w=None):
    d = x.shape[1]
    tr = _row_tile(256, dims)
    b0 = row0 // tr
    nblk = (x.shape[0] - row0) // tr
    in_specs = [pl.BlockSpec((tr, d), lambda i: (i + b0, 0)),
                pl.BlockSpec((1, d), lambda i: (0, 0)),
                pl.BlockSpec((1, 1, d), lambda i: (_grp(i + b0, tr, dims), 0, 0)),
                pl.BlockSpec((1, 1, d), lambda i: (_grp(i + b0, tr, dims), 0, 0))]
    args = [x, g.reshape(1, d), sh, sc]
    if router_w is None:
        out_specs = pl.BlockSpec((tr, d), lambda i: (i, 0))
        out_shape = jax.ShapeDtypeStruct((nblk * tr, d), BF)
    else:
        in_specs.append(pl.BlockSpec((d, LANES), lambda i: (0, 0)))
        args.append(router_w)
        out_specs = [pl.BlockSpec((tr, d), lambda i: (i, 0)), pl.BlockSpec((tr, LANES), lambda i: (i, 0))]
        out_shape = [jax.ShapeDtypeStruct((nblk * tr, d), F32), jax.ShapeDtypeStruct((nblk * tr, LANES), F32)]
    return pl.pallas_call(
        functools.partial(_norm_mod_kernel, with_router=router_w is not None),
        grid=(nblk,), in_specs=in_specs, out_specs=out_specs, out_shape=out_shape,
        compiler_params=_cparams(1),
        name="norm_mod_router" if router_w is not None else "norm_mod",
    )(*args)


def _mm_kernel(a_ref, b_ref, o_ref):
    o_ref[...] = _dot(a_ref[...], b_ref[...]).astype(o_ref.dtype)


def in_proj(h, w, dims):
    m, k = h.shape
    n = w.shape[1]
    tm = _row_tile(1024, dims)
    tn = 512
    return pl.pallas_call(
        _mm_kernel,
        grid=(m // tm, n // tn),
        in_specs=[pl.BlockSpec((tm, k), lambda i, j: (i, 0)),
                  pl.BlockSpec((k, tn), lambda i, j: (0, j))],
        out_specs=pl.BlockSpec((tm, tn), lambda i, j: (i, j)),
        out_shape=jax.ShapeDtypeStruct((m, n), BF),
        compiler_params=_cparams(2),
        name="in_proj",
    )(h, w)


def _qk_prep_kernel(za_ref, zb_ref, g_ref, cos_ref, sin_ref, o_ref):
    cos, sin = cos_ref[...], sin_ref[...]
    lane = lax.broadcasted_iota(jnp.int32, cos.shape, 1)
    first_half = (lane % (2 * ROPE_PAIRS)) < ROPE_PAIRS
    half_w = za_ref.shape[1]
    for h in range(o_ref.shape[1] // ATT_HD):
        c0 = h * ATT_HD
        src = za_ref if c0 < half_w else zb_ref
        x = src[:, c0 % half_w:c0 % half_w + ATT_HD].astype(F32)
        ms = jnp.mean(x * x, axis=-1, keepdims=True)
        y = x * lax.rsqrt(ms + 1e-6) * g_ref[:, c0:c0 + ATT_HD]
        partner = jnp.where(first_half, pltpu.roll(y, LANES - ROPE_PAIRS, 1), pltpu.roll(y, ROPE_PAIRS, 1))
        o_ref[:, c0:c0 + ATT_HD] = (y * cos + partner * sin).astype(BF)


def qk_prep(z, gains, cos_t, sin_t, dims):
    rows = z.shape[0]
    tr = _row_tile(256, dims, within_seq=True)
    qk_w = Q_COLS + KV_COLS
    half_w = qk_w // 2
    nctx = dims.r_ctx // tr
    pc, pl_ = dims.cl // tr, dims.s // tr

    def tab(i):
        return (jnp.where(i < nctx, i % pc, pc + (i - nctx) % pl_), 0)

    return pl.pallas_call(
        _qk_prep_kernel,
        grid=(rows // tr,),
        in_specs=[pl.BlockSpec((tr, half_w), lambda i: (i, Z_ATT // half_w)),
                  pl.BlockSpec((tr, half_w), lambda i: (i, Z_ATT // half_w + 1)),
                  pl.BlockSpec((1, qk_w), lambda i: (0, 0)),
                  pl.BlockSpec((tr, ATT_HD), tab),
                  pl.BlockSpec((tr, ATT_HD), tab)],
        out_specs=pl.BlockSpec((tr, qk_w), lambda i: (i, 0)),
        out_shape=jax.ShapeDtypeStruct((rows, qk_w), BF),
        compiler_params=_cparams(1),
        name="qk_prep",
    )(z, z, gains, cos_t, sin_t)


def _stack_heads(q):
    return jnp.concatenate([q[:, g * ATT_HD:(g + 1) * ATT_HD] for g in range(KV_GROUP)], axis=0)


def _attn_lat_kernel(q_ref, kp_ref, kc_ref, kn_ref, kx_ref, vp_ref, vc_ref, vn_ref, vx_ref, sink_ref, o_ref):
    n = pl.program_id(2)
    last = pl.num_programs(2) - 1
    q3 = _stack_heads(q_ref[...])
    s_p = _dot_nt(q3, kp_ref[...])
    s_c = _dot_nt(q3, kc_ref[...])
    s_n = _dot_nt(q3, kn_ref[...])
    s_x = _dot_nt(q3, kx_ref[...])
    i = lax.broadcasted_iota(jnp.int32, s_p.shape, 0) % ATT_BLOCK
    j = lax.broadcasted_iota(jnp.int32, s_p.shape, 1)
    s_p = jnp.where((j >= i) & (n > 0), s_p, NEG_INF)
    s_n = jnp.where((j <= i) & (n < last), s_n, NEG_INF)
    sk = sink_ref[0]
    m = jnp.maximum(jnp.maximum(jnp.max(s_p, axis=-1, keepdims=True), jnp.max(s_c, axis=-1, keepdims=True)),
                    jnp.maximum(jnp.max(s_n, axis=-1, keepdims=True), jnp.max(s_x, axis=-1, keepdims=True)))
    m = jnp.maximum(m, sk)
    p_p, p_c, p_n, p_x = jnp.exp(s_p - m), jnp.exp(s_c - m), jnp.exp(s_n - m), jnp.exp(s_x - m)
    den = (jnp.sum(p_p, axis=-1, keepdims=True) + jnp.sum(p_c, axis=-1, keepdims=True)
           + jnp.sum(p_n, axis=-1, keepdims=True) + jnp.sum(p_x, axis=-1, keepdims=True) + jnp.exp(sk - m))
    o = (_dot(p_p.astype(BF), vp_ref[...]) + _dot(p_c.astype(BF), vc_ref[...])
         + _dot(p_n.astype(BF), vn_ref[...]) + _dot(p_x.astype(BF), vx_ref[...])) / den
    for g in range(KV_GROUP):
        o_ref[:, g * ATT_HD:(g + 1) * ATT_HD] = o[g * ATT_BLOCK:(g + 1) * ATT_BLOCK].astype(BF)


def attn_latent(qk, z, sink_rows, dims):
    nblk = dims.s // ATT_BLOCK
    lat0 = dims.r_ctx // ATT_BLOCK
    kcol = Q_COLS // ATT_HD
    vcol = (Z_ATT + Q_COLS + KV_COLS) // ATT_HD

    def row(b, n):
        return lat0 + b * nblk + n

    def kv_specs(col0):
        return [pl.BlockSpec((ATT_BLOCK, ATT_HD), lambda b, g, n: (row(b, jnp.maximum(n - 1, 0)), col0 + g)),
                pl.BlockSpec((ATT_BLOCK, ATT_HD), lambda b, g, n: (row(b, n), col0 + g)),
                pl.BlockSpec((ATT_BLOCK, ATT_HD), lambda b, g, n: (row(b, jnp.minimum(n + 1, nblk - 1)), col0 + g)),
                pl.BlockSpec((dims.cl, ATT_HD), lambda b, g, n: (b, col0 + g))]

    gw = KV_GROUP * ATT_HD
    return pl.pallas_call(
        _attn_lat_kernel,
        grid=(dims.nb, KV_HEADS, nblk),
        in_specs=([pl.BlockSpec((ATT_BLOCK, gw), lambda b, g, n: (row(b, n), g))]
                  + kv_specs(kcol) + kv_specs(vcol)
                  + [pl.BlockSpec((1, KV_GROUP * ATT_BLOCK, 1), lambda b, g, n: (g, 0, 0))]),
        out_specs=pl.BlockSpec((ATT_BLOCK, gw), lambda b, g, n: (b * nblk + n, g)),
        out_shape=jax.ShapeDtypeStruct((dims.r_lat, Q_COLS), BF),
        compiler_params=_cparams(3),
        name="attn_latent",
    )(qk, qk, qk, qk, qk, z, z, z, z, sink_rows)


def _attn_ctx_kernel(q_ref, k_ref, v_ref, sink_ref, o_ref):
    q3 = _stack_heads(q_ref[...])
    s = _dot_nt(q3, k_ref[...])
    sk = sink_ref[0]
    m = jnp.maximum(jnp.max(s, axis=-1, keepdims=True), sk)
    p = jnp.exp(s - m)
    den = jnp.sum(p, axis=-1, keepdims=True) + jnp.exp(sk - m)
    o = _dot(p.astype(BF), v_ref[...]) / den
    cl = q_ref.shape[0]
    for g in range(KV_GROUP):
        o_ref[:, g * ATT_HD:(g + 1) * ATT_HD] = o[g * cl:(g + 1) * cl].astype(BF)


def attn_context(qk, z, sink_rows, dims):
    kcol = Q_COLS // ATT_HD
    vcol = (Z_ATT + Q_COLS + KV_COLS) // ATT_HD
    gw = KV_GROUP * ATT_HD
    return pl.pallas_call(
        _attn_ctx_kernel,
        grid=(dims.nb, KV_HEADS),
        in_specs=[pl.BlockSpec((dims.cl, gw), lambda b, g: (b, g)),
                  pl.BlockSpec((dims.cl, ATT_HD), lambda b, g: (b, kcol + g)),
                  pl.BlockSpec((dims.cl, ATT_HD), lambda b, g: (b, vcol + g)),
                  pl.BlockSpec((1, KV_GROUP * dims.cl, 1), lambda b, g: (g, 0, 0))],
        out_specs=pl.BlockSpec((dims.cl, gw), lambda b, g: (b, g)),
        out_shape=jax.ShapeDtypeStruct((dims.r_ctx, Q_COLS), BF),
        compiler_params=_cparams(2),
        name="attn_context",
    )(qk, z, z, sink_rows)


def _shift_rows(zf, prev_row, next_row):
    tt = zf.shape[0]
    row = lax.broadcasted_iota(jnp.int32, zf.shape, 0)
    prev = jnp.where(row == 0, prev_row, pltpu.roll(zf, 1, 0))
    nxt = jnp.where(row == tt - 1, next_row, pltpu.roll(zf, tt - 1, 0))
    return prev, nxt


def _rwkv_prep_kernel(z_ref, zp_ref, zn_ref, mup_ref, mun_ref, w0_ref, w2_ref, a0_ref, a2_ref, g2_ref, kk_ref,
                      r_o, k_o, v_o, kk_o, a0_o, a1_o, lw0_o, lw1_o, g_o, *, dims):
    i = pl.program_id(0)
    tt = z_ref.shape[0]
    pos, cnt = _seq_pos(i, tt, dims)
    zf = z_ref[...].astype(F32)
    hb = zp_ref.shape[0]
    prev_row = jnp.where(pos > 0, zp_ref[...].astype(F32)[hb - 1:hb, :], 0.0)
    next_row = jnp.where(pos < cnt - 1, zn_ref[...].astype(F32)[0:1, :], 0.0)
    prev, nxt = _shift_rows(zf, prev_row, next_row)
    zs = zf + (prev - zf) * mup_ref[...] + (nxt - zf) * mun_ref[...]
    w = RWKV_WIDTH
    r = zs[:, Z_R:Z_R + w]
    k = zs[:, Z_K:Z_K + w]
    v = zs[:, Z_V:Z_V + w]
    wd = zs[:, Z_WD:Z_WD + 2 * DECAY_LORA]
    ad = zs[:, Z_AD:Z_AD + 2 * ICLR_LORA]
    gd = zs[:, Z_GD:Z_GD + GATE_PAD]
    lw = -math.exp(-0.5) * _sigmoid(w0_ref[...] + _dot(jnp.tanh(wd).astype(BF), w2_ref[...]))
    a = _sigmoid(a0_ref[...] + _dot(ad.astype(BF), a2_ref[...]))
    g = _dot(_sigmoid(gd).astype(BF), g2_ref[...])
    kq = k * kk_ref[...]
    ssq = _seg_sum(kq * kq, _seg_ones(RWKV_HD))
    kk = kq * lax.rsqrt(jnp.maximum(ssq, 1e-24))
    r_o[...] = r.astype(BF)
    k_o[...] = k.astype(BF)
    v_o[...] = v.astype(BF)
    kk_o[...] = kk.astype(BF)
    a0_o[...] = a[:, :w].astype(BF)
    a1_o[...] = a[:, w:].astype(BF)
    lw0_o[...] = lw[:, :w]
    lw1_o[...] = lw[:, w:]
    g_o[...] = g.astype(BF)


def rwkv_prep(z, p, dims):
    rows = z.shape[0]
    tt = _row_tile(256, dims, within_seq=True)
    hb = SUBLANES_BF16
    nhb = rows // hb
    w = RWKV_WIDTH
    full = lambda shape: pl.BlockSpec(shape, lambda i: (0,) * len(shape))
    out_bf = jax.ShapeDtypeStruct((rows, w), BF)
    out_f = jax.ShapeDtypeStruct((rows, w), F32)
    ospec = pl.BlockSpec((tt, w), lambda i: (i, 0))
    return pl.pallas_call(
        functools.partial(_rwkv_prep_kernel, dims=dims),
        grid=(rows // tt,),
        in_specs=[pl.BlockSpec((tt, RWKV_PAD), lambda i: (i, 0)),
                  pl.BlockSpec((hb, RWKV_PAD), lambda i: (jnp.maximum(i * (tt // hb) - 1, 0), 0)),
                  pl.BlockSpec((hb, RWKV_PAD), lambda i: (jnp.minimum((i + 1) * (tt // hb), nhb - 1), 0)),
                  full((1, RWKV_PAD)), full((1, RWKV_PAD)),
                  full((1, 2 * w)), full((2 * DECAY_LORA, 2 * w)),
                  full((1, 2 * w)), full((2 * ICLR_LORA, 2 * w)),
                  full((GATE_PAD, w)), full((1, w))],
        out_specs=[ospec] * 9,
        out_shape=[out_bf, out_bf, out_bf, out_bf, out_bf, out_bf, out_f, out_f, out_bf],
        compiler_params=_cparams(1),
        name="rwkv_prep",
    )(z, z, z, p['mu_prev'], p['mu_next'], p['w0'], p['w2'], p['a0'], p['a2'], p['g2'], p['k_k'])


def _bmm(a, b):
    return lax.dot_general(a.astype(BF), b.astype(BF), (((2,), (1,)), ((0,), (0,))), preferred_element_type=F32)


def _bmm_nt(a, b):
    return lax.dot_general(a.astype(BF), b.astype(BF), (((2,), (2,)), ((0,), (0,))), preferred_element_type=F32)


def _bmm_nt_hp(a, b):
    ah = a.astype(BF)
    al = (a - ah.astype(F32)).astype(BF)
    bh = b.astype(BF)
    bl = (b - bh.astype(F32)).astype(BF)
    return _bmm_nt(ah, bh) + (_bmm_nt(ah, bl) + _bmm_nt(al, bh))


def _tri_inverse(lb):
    c = lb.shape[-1]
    row = lax.broadcasted_iota(jnp.int32, (c, c), 0)
    col = lax.broadcasted_iota(jnp.int32, (c, c), 1)
    joins = lambda m: ((row // (2 * m)) == (col // (2 * m))) & ((row // m) != (col // m))
    t = jnp.where(row == col, 1.0, 0.0) - jnp.where(joins(1), lb, 0.0)
    m = 2
    while m < c:
        t = t - _bmm(_bmm(t, jnp.where(joins(m), lb, 0.0)), t)
        m *= 2
    return t


def _head_stack(x, head_of_lane):
    heads = x.shape[1] // RWKV_HD
    return jnp.concatenate([jnp.where(head_of_lane == h, x, 0.0) for h in range(heads)], axis=0).astype(BF)


def _scan_group_operands(r, k, v, kk, a, lw, ka, s_bd, reverse):
    c, wdt = r.shape
    heads = wdt // RWKV_HD
    row = lax.broadcasted_iota(jnp.int32, (c, c), 0)
    col = lax.broadcasted_iota(jnp.int32, (c, c), 1)
    before = (col > row) if reverse else (col < row)
    upto = (col >= row) if reverse else (col <= row)
    kmod = k * (1.0 + (a - 1.0) * ka)
    b = kk * a
    cum = _dot_exact_lhs(jnp.where(upto, 1.0, 0.0).astype(BF), lw)
    tot = jnp.sum(lw, axis=0, keepdims=True)
    g_in, g_ex, g_inv, g_end = jnp.exp(cum), jnp.exp(cum - lw), jnp.exp(-cum), jnp.exp(tot - cum)
    kap = kk * g_ex
    r_t = r * g_in
    km_t = (kmod * g_inv).astype(BF)
    b_t = (b * g_inv).astype(BF)
    head_of_lane = lax.broadcasted_iota(jnp.int32, (c, wdt), 1) // RWKV_HD
    lhs = jnp.concatenate([_head_stack(kap, head_of_lane), _head_stack(r_t, head_of_lane)], axis=0)
    pk = _dot_nt(lhs, km_t).reshape(2 * heads, c, c)
    pb = _dot_nt(lhs, b_t).reshape(2 * heads, c, c)
    sb = s_bd.astype(BF)
    return {
        'lk': jnp.where(before, pk[:heads], 0.0), 'mk': jnp.where(upto, pk[heads:], 0.0),
        'lb': jnp.where(before, pb[:heads], 0.0), 'mb': jnp.where(upto, pb[heads:], 0.0),
        'sk': _dot_nt(sb, kap.astype(BF)).reshape(heads, RWKV_HD, c),
        'sr': _dot_nt(sb, r_t.astype(BF)).reshape(heads, RWKV_HD, c),
        'vt': v.T.reshape(heads, RWKV_HD, c),
        'km_e': (kmod * g_end).astype(BF), 'b_e': (b * g_end).astype(BF), 'g_tot': jnp.exp(tot),
    }


def _rwkv_scan_kernel(rf, kf, vf, kkf, af, lwf, rr, kr, vr, kkr, ar, lwr, ka_ref, y0_ref, y1_ref, s_ref):
    @pl.when(pl.program_id(2) == 0)
    def _():
        s_ref[...] = jnp.zeros(s_ref.shape, F32)

    gw = s_ref.shape[-1]
    heads = gw // RWKV_HD
    n_groups = s_ref.shape[1]
    c = rf.shape[0]
    dirs = ((rf, kf, vf, kkf, af, lwf, False), (rr, kr, vr, kkr, ar, lwr, True))
    groups = []
    for d, (r_ref, k_ref, v_ref, kk_ref, a_ref, lw_ref, reverse) in enumerate(dirs):
        for q in range(n_groups):
            sl = slice(q * gw, (q + 1) * gw)
            ld = lambda ref: ref[:, sl].astype(F32)
            s_bd = s_ref[d, q]
            ops = _scan_group_operands(ld(r_ref), ld(k_ref), ld(v_ref), ld(kk_ref), ld(a_ref), lw_ref[:, sl],
                                       ka_ref[:, sl], s_bd, reverse)
            ops['s_bd'] = s_bd
            groups.append(ops)

    cat = lambda key: jnp.concatenate([g[key] for g in groups], axis=0)
    lk, mk, lb, mb, sk, sr, vt = (cat(key) for key in ('lk', 'mk', 'lb', 'mb', 'sk', 'sr', 'vt'))
    xt = sk + _bmm_nt(vt, lk)
    t0 = _tri_inverse(lb)
    u0 = _bmm_nt(xt, t0)
    res = xt - u0 - _bmm_nt_hp(u0, lb)
    ut = u0 + _bmm_nt(res, t0)
    yt = sr + _bmm_nt(vt, mk) - _bmm_nt(ut, mb)

    row_head = lax.broadcasted_iota(jnp.int32, (gw, gw), 0) // RWKV_HD
    col_head = lax.broadcasted_iota(jnp.int32, (gw, gw), 1) // RWKV_HD
    for i, g in enumerate(groups):
        d, q = divmod(i, n_groups)
        sl = slice(q * gw, (q + 1) * gw)
        hs = slice(i * heads, (i + 1) * heads)
        y = yt[hs].reshape(gw, c).T
        (y0_ref, y1_ref)[d][:, sl] = y
        vts = vt[hs].reshape(gw, c).astype(BF)
        uts = ut[hs].reshape(gw, c).astype(BF)
        upd = g['s_bd'] * g['g_tot'] + _dot(vts, g['km_e']) - _dot(uts, g['b_e'])
        s_ref[d, q] = jnp.where(row_head == col_head, upd, 0.0)


def rwkv_scan(r, k, v, kk, a0, a1, lw0, lw1, k_a, dims):
    c = SCAN_CHUNK
    ncc, nlc = dims.cl // c, dims.s // c
    lat0 = dims.r_ctx // c
    gw = SCAN_GROUP_HEADS * RWKV_HD
    wdt = SCAN_GROUPS * gw

    def fwd(b, hb, t):
        return (jnp.where(t < ncc, b * ncc + t, lat0 + b * nlc + (t - ncc)), hb)

    def rev(b, hb, t):
        return (jnp.where(t < ncc, b * ncc + (ncc - 1 - t), lat0 + b * nlc + (nlc - 1 - (t - ncc))), hb)

    fs = pl.BlockSpec((c, wdt), fwd)
    rs = pl.BlockSpec((c, wdt), rev)
    y_shape = jax.ShapeDtypeStruct(lw0.shape, F32)
    return pl.pallas_call(
        _rwkv_scan_kernel,
        grid=(dims.nb, RWKV_WIDTH // wdt, ncc + nlc),
        in_specs=[fs] * 6 + [rs] * 6 + [pl.BlockSpec((1, wdt), lambda b, hb, t: (0, hb))],
        out_specs=[fs, rs],
        out_shape=[y_shape, y_shape],
        scratch_shapes=[pltpu.VMEM((2, SCAN_GROUPS, gw, gw), F32)],
        compiler_params=_cparams(3),
        name="rwkv_scan",
    )(r, k, v, kk, a0, lw0, r, k, v, kk, a1, lw1, k_a)


def _rwkv_out_kernel(y0_ref, y1_ref, r_ref, k_ref, v_ref, a0_ref, a1_ref, g_ref, ka_ref, rk_ref, lng_ref, lnb_ref,
                     o_ref):
    seg = _seg_ones(RWKV_HD)
    y = y0_ref[...] + y1_ref[...]
    mu = _seg_sum(y, seg) * (1.0 / RWKV_HD)
    yc = y - mu
    var = _seg_sum(yc * yc, seg) * (1.0 / RWKV_HD)
    yn = yc * lax.rsqrt(var + RWKV_GN_EPS) * lng_ref[...] + lnb_ref[...]
    ld = lambda ref: ref[...].astype(F32)
    r, k, v = ld(r_ref), ld(k_ref), ld(v_ref)
    kmod_sum = k * (2.0 + (ld(a0_ref) + ld(a1_ref) - 2.0) * ka_ref[...])
    bonus = _seg_sum(r * kmod_sum * rk_ref[...], seg) * v
    o_ref[...] = ((yn + bonus) * ld(g_ref)).astype(BF)


def rwkv_out(y0, y1, r, k, v, a0, a1, g, p, dims, row0=0):
    rows, w = y0.shape
    tr = _row_tile(256, dims)
    b0 = row0 // tr
    rs = pl.BlockSpec((tr, w), lambda i: (i + b0, 0))
    ps = pl.BlockSpec((1, w), lambda i: (0, 0))
    return pl.pallas_call(
        _rwkv_out_kernel,
        grid=((rows - row0) // tr,),
        in_specs=[rs] * 8 + [ps] * 4,
        out_specs=pl.BlockSpec((tr, w), lambda i: (i, 0)),
        out_shape=jax.ShapeDtypeStruct((rows - row0, w), BF),
        compiler_params=_cparams(1),
        name="rwkv_out",
    )(y0, y1, r, k, v, a0, a1, g, p['k_a'], p['r_k'], p['ln_g'], p['ln_b'])


def _conv_kernel(val_ref, gate_ref, vp_ref, gp_ref, vn_ref, gn_ref, w_ref, b_ref, lng_ref, lnb_ref, o_ref,
                 u_ref, acc_ref, *, dims, b0):
    i = pl.program_id(0) + b0
    tt = val_ref.shape[0]
    hb = vp_ref.shape[0]
    pos, cnt = _seq_pos(i, tt, dims)
    glu = lambda a, b: a.astype(F32) * _sigmoid(b.astype(F32))
    u_ref[0:hb, :] = jnp.where(pos > 0, glu(vp_ref[...], gp_ref[...]), 0.0)
    u_ref[hb:hb + tt, :] = glu(val_ref[...], gate_ref[...])
    u_ref[hb + tt:2 * hb + tt, :] = jnp.where(pos < cnt - 1, glu(vn_ref[...], gn_ref[...]), 0.0)
    half = CONV_K // 2
    for c0 in range(0, CONV_CH, LANES):
        acc = jnp.zeros((tt, LANES), F32) + b_ref[:, c0:c0 + LANES]
        for j in range(CONV_K):
            off = hb - half + j
            acc = acc + u_ref[off:off + tt, c0:c0 + LANES] * w_ref[j:j + 1, c0:c0 + LANES]
        acc_ref[:, c0:c0 + LANES] = acc
    y = acc_ref[...]
    mu = jnp.mean(y, axis=-1, keepdims=True)
    yc = y - mu
    var = jnp.mean(yc * yc, axis=-1, keepdims=True)
    yn = yc * lax.rsqrt(var + 1e-5) * lng_ref[...] + lnb_ref[...]
    o_ref[...] = (yn * _sigmoid(yn)).astype(BF)


def conformer_conv(z, p, dims, row0=0):
    rows = z.shape[0]
    tt = _row_tile(256, dims, within_seq=True)
    hb = SUBLANES_BF16
    nhb = rows // hb
    b0 = row0 // tt
    vcol, gcol = Z_CONV // CONV_CH, Z_CONV // CONV_CH + 1
    prev = lambda i: jnp.maximum((i + b0) * (tt // hb) - 1, 0)
    nxt = lambda i: jnp.minimum((i + b0 + 1) * (tt // hb), nhb - 1)
    ps = pl.BlockSpec((1, CONV_CH), lambda i: (0, 0))
    return pl.pallas_call(
        functools.partial(_conv_kernel, dims=dims, b0=b0),
        grid=((rows - row0) // tt,),
        in_specs=[pl.BlockSpec((tt, CONV_CH), lambda i: (i + b0, vcol)),
                  pl.BlockSpec((tt, CONV_CH), lambda i: (i + b0, gcol)),
                  pl.BlockSpec((hb, CONV_CH), lambda i: (prev(i), vcol)),
                  pl.BlockSpec((hb, CONV_CH), lambda i: (prev(i), gcol)),
                  pl.BlockSpec((hb, CONV_CH), lambda i: (nxt(i), vcol)),
                  pl.BlockSpec((hb, CONV_CH), lambda i: (nxt(i), gcol)),
                  pl.BlockSpec((CONV_K + 1, CONV_CH), lambda i: (0, 0)),
                  ps, ps, ps],
        out_specs=pl.BlockSpec((tt, CONV_CH), lambda i: (i, 0)),
        out_shape=jax.ShapeDtypeStruct((rows - row0, CONV_CH), BF),
        scratch_shapes=[pltpu.VMEM((tt + 2 * hb, CONV_CH), F32), pltpu.VMEM((tt, CONV_CH), F32)],
        compiler_params=_cparams(1),
        name="conformer_conv",
    )(z, z, z, z, z, z, p['conv_w'], p['conv_b'], p['conv_ln_g'], p['conv_ln_b'])


def _out_proj_kernel(*refs, n_ctx_blocks):
    if n_ctx_blocks:
        ac_ref, al_ref, r_ref, c_ref, wa_ref, wr_ref, wc_ref, x_ref, g_ref, o_ref = refs
        a = jnp.where(pl.program_id(0) < n_ctx_blocks, ac_ref[...], al_ref[...])
    else:
        al_ref, r_ref, c_ref, wa_ref, wr_ref, wc_ref, x_ref, g_ref, o_ref = refs
        a = al_ref[...]
    acc = _dot(a, wa_ref[...]) + _dot(r_ref[...], wr_ref[...]) + _dot(c_ref[...], wc_ref[...])
    o_ref[...] = x_ref[...] + g_ref[0] * acc


def out_proj(a_ctx, a_lat, rw, cv, w_out, x, gate, dims, row0=0):
    rows, d = x.shape
    tm = _row_tile(1024, dims)
    tn = 512
    b0 = row0 // tm
    nctx = dims.r_ctx // tm
    n_ctx_blocks = nctx - b0
    assert (a_ctx is not None) == (n_ctx_blocks > 0)
    a_specs, a_args = [], []
    if n_ctx_blocks:
        a_specs.append(pl.BlockSpec((tm, Q_COLS), lambda i, j: (jnp.minimum(i, nctx - 1), 0)))
        a_args.append(a_ctx)
    a_specs.append(pl.BlockSpec((tm, Q_COLS), lambda i, j: (jnp.maximum(i + b0 - nctx, 0), 0)))
    a_args.append(a_lat)
    return pl.pallas_call(
        functools.partial(_out_proj_kernel, n_ctx_blocks=n_ctx_blocks),
        grid=((rows - row0) // tm, d // tn),
        in_specs=a_specs + [
                  pl.BlockSpec((tm, RWKV_WIDTH), lambda i, j: (i, 0)),
                  pl.BlockSpec((tm, CONV_CH), lambda i, j: (i, 0)),
                  pl.BlockSpec((Q_COLS, tn), lambda i, j: (0, j)),
                  pl.BlockSpec((RWKV_WIDTH, tn), lambda i, j: (0, j)),
                  pl.BlockSpec((CONV_CH, tn), lambda i, j: (0, j)),
                  pl.BlockSpec((tm, tn), lambda i, j: (i + b0, j)),
                  pl.BlockSpec((1, 1, tn), lambda i, j: (_grp(i + b0, tm, dims), 0, j))],
        out_specs=pl.BlockSpec((tm, tn), lambda i, j: (i + b0, j)),
        out_shape=jax.ShapeDtypeStruct(x.shape, F32),
        input_output_aliases={len(a_args) + 5: 0},
        compiler_params=_cparams(2),
        name="out_proj",
    )(*a_args, rw, cv, w_out[:Q_COLS], w_out[Q_COLS:Q_COLS + RWKV_WIDTH], w_out[Q_COLS + RWKV_WIDTH:], x, gate)


def _row_gather(idx_ref, n, src_hbm, dst, sem):
    def body(r, carry):
        t = idx_ref[0, 0, r]
        pltpu.make_async_copy(src_hbm.at[pl.ds(t, 1)], dst.at[pl.ds(r, 1)], sem).start()
        return carry
    lax.fori_loop(0, n, body, 0, unroll=8)


def _moe_kernel(be_ref, act_ref, tok_ref, tokn_ref, h_hbm, wg_ref, wu_ref, wd_ref, o_ref, xbuf, sem):
    del be_ref
    i = pl.program_id(0)
    nblk = pl.num_programs(0)
    slot = i % 2
    rows = xbuf.shape[1]

    @pl.when(i == 0)
    def _():
        _row_gather(tok_ref, rows, h_hbm, xbuf.at[0], sem.at[0])

    @pl.when(i + 1 < nblk)
    def _():
        _row_gather(tokn_ref, rows, h_hbm, xbuf.at[1 - slot], sem.at[1 - slot])

    pltpu.make_async_copy(h_hbm.at[pl.ds(0, rows)], xbuf.at[slot], sem.at[slot]).wait()

    @pl.when(act_ref[i] > 0)
    def _():
        xb = xbuf[slot].astype(BF)
        gate = _dot(xb, wg_ref[0])
        up = _dot(xb, wu_ref[0])
        hid = (gate * _sigmoid(gate) * up).astype(BF)
        o_ref[...] = _dot(hid, wd_ref[0])

    @pl.when(act_ref[i] == 0)
    def _():
        o_ref[...] = jnp.zeros(o_ref.shape, F32)


def moe_experts(h2, blk_expert, blk_active, slot_tok, w_gate, w_up, w_down):
    n_blk = blk_expert.shape[0]
    d = h2.shape[1]
    de = w_gate.shape[2]
    b = MOE_ROWS
    tok3 = slot_tok.reshape(n_blk, 1, b)
    grid_spec = pltpu.PrefetchScalarGridSpec(
        num_scalar_prefetch=2,
        grid=(n_blk,),
        in_specs=[pl.BlockSpec((1, 1, b), lambda i, be, act: (i, 0, 0), memory_space=pltpu.SMEM),
                  pl.BlockSpec((1, 1, b), lambda i, be, act: (jnp.minimum(i + 1, n_blk - 1), 0, 0),
                               memory_space=pltpu.SMEM),
                  pl.BlockSpec(memory_space=pl.ANY),
                  pl.BlockSpec((1, d, de), lambda i, be, act: (be[i], 0, 0)),
                  pl.BlockSpec((1, d, de), lambda i, be, act: (be[i], 0, 0)),
                  pl.BlockSpec((1, de, d), lambda i, be, act: (be[i], 0, 0))],
        out_specs=pl.BlockSpec((b, d), lambda i, be, act: (i, 0)),
        scratch_shapes=[pltpu.VMEM((2, b, d), F32), pltpu.SemaphoreType.DMA((2,))],
    )
    return pl.pallas_call(
        _moe_kernel,
        grid_spec=grid_spec,
        out_shape=jax.ShapeDtypeStruct((n_blk * b, d), F32),
        compiler_params=_cparams(1),
        name="moe_experts",
    )(blk_expert, blk_active, tok3, tok3, h2, w_gate, w_up, w_down)


def _combine_kernel(pos_ref, posn_ref, o_hbm, wt_ref, x_ref, g_ref, out_ref, buf, sem):
    i = pl.program_id(0)
    nblk = pl.num_programs(0)
    slot = i % 2
    n2 = buf.shape[1]

    @pl.when(i == 0)
    def _():
        _row_gather(pos_ref, n2, o_hbm, buf.at[0], sem.at[0])

    @pl.when(i + 1 < nblk)
    def _():
        _row_gather(posn_ref, n2, o_hbm, buf.at[1 - slot], sem.at[1 - slot])

    pltpu.make_async_copy(o_hbm.at[pl.ds(0, n2)], buf.at[slot], sem.at[slot]).wait()
    tb = n2 // TOP_K
    wt = wt_ref[...]
    y = buf[slot, 0:tb, :] * wt[:, 0:1] + buf[slot, tb:n2, :] * wt[:, 1:2]
    out_ref[...] = x_ref[...] + g_ref[0] * y


def moe_combine(o_slots, pos, wts, x, gate, dims, row0=0, drop_head_rows=False):
    rows, d = x.shape
    tb = COMBINE_ROWS
    n_blk = (rows - row0) // tb
    b0 = row0 // tb
    out_row0 = row0 if drop_head_rows else 0
    ob0 = b0 - out_row0 // tb
    posb = pos.reshape(n_blk, tb, TOP_K).transpose(0, 2, 1).reshape(n_blk, 1, TOP_K * tb)
    return pl.pallas_call(
        _combine_kernel,
        grid=(n_blk,),
        in_specs=[pl.BlockSpec((1, 1, TOP_K * tb), lambda i: (i, 0, 0), memory_space=pltpu.SMEM),
                  pl.BlockSpec((1, 1, TOP_K * tb), lambda i: (jnp.minimum(i + 1, n_blk - 1), 0, 0),
                               memory_space=pltpu.SMEM),
                  pl.BlockSpec(memory_space=pl.ANY),
                  pl.BlockSpec((tb, TOP_K), lambda i: (i, 0)),
                  pl.BlockSpec((tb, d), lambda i: (i + b0, 0)),
                  pl.BlockSpec((1, 1, d), lambda i: (_grp(i + b0, tb, dims), 0, 0))],
        out_specs=pl.BlockSpec((tb, d), lambda i: (i + ob0, 0)),
        out_shape=jax.ShapeDtypeStruct((rows - out_row0, d), F32),
        scratch_shapes=[pltpu.VMEM((2, TOP_K * tb, d), F32), pltpu.SemaphoreType.DMA((2,))],
        input_output_aliases={} if drop_head_rows else {4: 0},
        compiler_params=_cparams(1),
        name="moe_combine",
    )(posb, posb, o_slots, wts, x, gate)


def moe_routing(logits, router_b):
    n = logits.shape[0]
    scores = jax.nn.sigmoid(logits[:, :N_EXPERTS])
    sel = (scores + router_b.astype(F32)).reshape(n, N_GROUPS, EXPERTS_PER_GROUP)
    grp_score = jnp.sum(lax.top_k(sel, TOP_K)[0], axis=-1)
    g_idx = jnp.argmax(grp_score, axis=-1)
    in_grp = jnp.take_along_axis(sel, g_idx[:, None, None], axis=1)[:, 0]
    _, local = lax.top_k(in_grp, TOP_K)
    expert = g_idx[:, None] * EXPERTS_PER_GROUP + local
    wts = jnp.take_along_axis(scores, expert, axis=1)
    wts = wts / jnp.sum(wts, axis=-1, keepdims=True)

    b = MOE_ROWS
    nk = n * TOP_K
    e_flat = expert.reshape(-1).astype(jnp.int32)
    pos_flat, base = expert_slots(e_flat)
    pend = base[1:N_EXPERTS + 1]
    n_blocks = -(-(nk + N_EXPERTS * (b - 1)) // b)
    tok = jnp.arange(nk, dtype=jnp.int32) // TOP_K
    slot_tok = jnp.zeros((n_blocks * b,), jnp.int32).at[pos_flat].set(tok)
    blk_start = jnp.arange(n_blocks, dtype=jnp.int32) * b
    blk_expert = jnp.minimum(jnp.sum((pend[None, :] <= blk_start[:, None]).astype(jnp.int32), axis=1),
                             N_EXPERTS - 1)
    blk_active = (blk_start < pend[-1]).astype(jnp.int32)
    return blk_expert, blk_active, slot_tok, wts, pos_flat.reshape(n, TOP_K)


def _expert_slots_kernel(e_ref, pos_ref, base_ref, cnt_ref, *, block_rows):
    ph = pl.program_id(0)
    j = pl.program_id(1)
    e = e_ref[0]
    tb = e.shape[1]
    onehot = jnp.where(lax.broadcasted_iota(jnp.int32, (LANES, tb), 0) == e, 1.0, 0.0)
    in_block = jnp.sum(onehot, axis=1, keepdims=True)

    @pl.when((ph == 0) & (j == 0))
    def _():
        cnt_ref[...] = jnp.zeros(cnt_ref.shape, F32)

    @pl.when(ph == 0)
    def _():
        cnt_ref[...] += in_block

    @pl.when((ph == 1) & (j == 0))
    def _():
        padded = jnp.floor((cnt_ref[...] + (block_rows - 1)) * (1.0 / block_rows)) * block_rows
        row = lax.broadcasted_iota(jnp.int32, (LANES, LANES), 0)
        col = lax.broadcasted_iota(jnp.int32, (LANES, LANES), 1)
        earlier = jnp.where(col < row, 1.0, 0.0).astype(BF)
        base_ref[...] = _dot_exact_lhs(earlier, jnp.broadcast_to(padded, (LANES, LANES)))[:, 0:1]
        cnt_ref[...] = jnp.zeros(cnt_ref.shape, F32)

    @pl.when(ph == 1)
    def _():
        row = lax.broadcasted_iota(jnp.int32, (tb, tb), 0)
        col = lax.broadcasted_iota(jnp.int32, (tb, tb), 1)
        before = jnp.where(row < col, 1.0, 0.0).astype(BF)
        rank = _dot(onehot.astype(BF), before)
        slot = jnp.sum(onehot * (base_ref[...] + cnt_ref[...] + rank), axis=0, keepdims=True)
        pos_ref[0] = slot.astype(jnp.int32)
        cnt_ref[...] += in_block


def expert_slots(e_flat):
    nk = e_flat.shape[0]
    tb = math.gcd(nk, 512)
    n_blk = nk // tb
    pos, base = pl.pallas_call(
        functools.partial(_expert_slots_kernel, block_rows=MOE_ROWS),
        grid=(2, n_blk),
        in_specs=[pl.BlockSpec((1, 1, tb), lambda ph, j: (j, 0, 0))],
        out_specs=[pl.BlockSpec((1, 1, tb), lambda ph, j: (ph * j, 0, 0)),
                   pl.BlockSpec((LANES, 1), lambda ph, j: (0, 0))],
        out_shape=[jax.ShapeDtypeStruct((n_blk, 1, tb), jnp.int32), jax.ShapeDtypeStruct((LANES, 1), F32)],
        scratch_shapes=[pltpu.VMEM((LANES, 1), F32)],
        compiler_params=_cparams(2),
        name="expert_slots",
    )(e_flat.reshape(n_blk, 1, tb))
    return pos.reshape(nk), base[:, 0].astype(jnp.int32)


def _rope_tables(dims):
    rows = dims.s // GRID_W
    row = jnp.repeat(jnp.arange(rows), GRID_W).astype(F32)
    col = jnp.tile(jnp.arange(GRID_W), rows).astype(F32)
    inv = ROPE_BASE ** (-jnp.arange(ROPE_PAIRS, dtype=F32) / ROPE_PAIRS)
    ang = jnp.concatenate([row[:, None] * inv, row[:, None] * inv, col[:, None] * inv, col[:, None] * inv], axis=1)
    sign = jnp.tile(jnp.concatenate([-jnp.ones((ROPE_PAIRS,), F32), jnp.ones((ROPE_PAIRS,), F32)]), 2)
    cos_t = jnp.concatenate([jnp.ones((dims.cl, ATT_HD), F32), jnp.cos(ang)], axis=0)
    sin_t = jnp.concatenate([jnp.zeros((dims.cl, ATT_HD), F32), jnp.sin(ang) * sign], axis=0)
    return cos_t, sin_t


def _block_diag2(w):
    z = jnp.zeros_like(w[0])
    return jnp.concatenate([jnp.concatenate([w[0], z], axis=1), jnp.concatenate([z, w[1]], axis=1)], axis=0)


def _pad_cols(v, n):
    return jnp.pad(v, [(0, 0)] * (v.ndim - 1) + [(0, n - v.shape[-1])])


def _layer_params(l, w_in, q_norm_g, k_norm_g, attn_sink, rwkv_mu_prev, rwkv_mu_next, rwkv_w0, rwkv_w2, rwkv_a0,
                  rwkv_a2, rwkv_g2, rwkv_k_k, rwkv_k_a, rwkv_r_k, rwkv_ln_g, rwkv_ln_b, conv_w, conv_b, conv_ln_g,
                  conv_ln_b, dims):
    d = w_in.shape[1]
    wl = w_in[l]
    w_in_p = jnp.concatenate([wl[:, ATT_COLS:ATT_COLS + RWKV_COLS], jnp.zeros((d, RWKV_PAD - RWKV_COLS), F32),
                              wl[:, ATT_COLS + RWKV_COLS:], wl[:, :ATT_COLS]], axis=1).astype(BF)
    scale = ATT_HD ** -0.5
    gains = jnp.concatenate([jnp.tile(q_norm_g[l] * scale, (ATT_HEADS, 1)), jnp.tile(k_norm_g[l], (KV_HEADS, 1))],
                            axis=0).reshape(1, Q_COLS + KV_COLS)
    sink = attn_sink[l].astype(F32).reshape(KV_HEADS, KV_GROUP, 1)
    return {
        'w_in': w_in_p,
        'gains': gains,
        'sink_lat': jnp.repeat(sink, ATT_BLOCK, axis=1).reshape(KV_HEADS, KV_GROUP * ATT_BLOCK, 1),
        'sink_ctx': jnp.repeat(sink, dims.cl, axis=1).reshape(KV_HEADS, KV_GROUP * dims.cl, 1),
        'mu_prev': _pad_cols(rwkv_mu_prev[l][None], RWKV_PAD),
        'mu_next': _pad_cols(rwkv_mu_next[l][None], RWKV_PAD),
        'w0': rwkv_w0[l].reshape(1, 2 * RWKV_WIDTH),
        'w2': _block_diag2(rwkv_w2[l]).astype(BF),
        'a0': rwkv_a0[l].reshape(1, 2 * RWKV_WIDTH),
        'a2': _block_diag2(rwkv_a2[l]).astype(BF),
        'g2': jnp.pad(rwkv_g2[l], ((0, GATE_PAD - GATE_LORA), (0, 0))).astype(BF),
        'k_k': rwkv_k_k[l][None],
        'k_a': rwkv_k_a[l][None],
        'r_k': rwkv_r_k[l].reshape(1, RWKV_WIDTH),
        'ln_g': rwkv_ln_g[l][None],
        'ln_b': rwkv_ln_b[l][None],
        'conv_w': jnp.pad(conv_w[l], ((0, 1), (0, 0))),
        'conv_b': conv_b[l][None],
        'conv_ln_g': conv_ln_g[l][None],
        'conv_ln_b': conv_ln_b[l][None],
    }


def kernel(x, c, ctx, c_ctx, ada_w, ada_b, norm1_g, norm2_g, w_in, q_norm_g, k_norm_g, attn_sink, rwkv_mu_prev, rwkv_mu_next, rwkv_w0, rwkv_w2, rwkv_a0, rwkv_a2, rwkv_g2, rwkv_k_k, rwkv_k_a, rwkv_r_k, rwkv_ln_g, rwkv_ln_b, conv_w, conv_b, conv_ln_g, conv_ln_b, w_out, router_w, router_b, moe_w_gate, moe_w_up, moe_w_down):
    nb, s, d = x.shape
    dims = Dims(nb=nb, s=s, cl=ctx.shape[1])
    n_layers = w_in.shape[0]
    mod_rows = -(-(nb + 1) // 8) * 8
    cs = jnp.zeros((mod_rows, d), F32).at[:nb].set(c).at[nb].set(c_ctx)
    mods = ada_tables(cs, ada_w, ada_b).reshape(n_layers, mod_rows, 6, 1, d)
    cos_t, sin_t = _rope_tables(dims)
    router_p = _pad_cols(router_w.astype(F32), LANES)
    xa = jnp.concatenate([ctx.reshape(dims.r_ctx, d), x.reshape(dims.r_lat, d)], axis=0)

    for l in range(n_layers):
        ctx_out = l < n_layers - 1
        row0 = 0 if ctx_out else dims.r_ctx
        p = _layer_params(l, w_in, q_norm_g, k_norm_g, attn_sink, rwkv_mu_prev, rwkv_mu_next, rwkv_w0, rwkv_w2,
                          rwkv_a0, rwkv_a2, rwkv_g2, rwkv_k_k, rwkv_k_a, rwkv_r_k, rwkv_ln_g, rwkv_ln_b, conv_w,
                          conv_b, conv_ln_g, conv_ln_b, dims)
        sh1, sc1, g1, sh2, sc2, g2 = [mods[l, :, m] for m in range(6)]

        h1 = norm_mod(xa, norm1_g[l], sh1, sc1, dims)
        z = in_proj(h1, p['w_in'], dims)

        qk = qk_prep(z, p['gains'], cos_t, sin_t, dims)
        att_lat = attn_latent(qk, z, p['sink_lat'], dims)
        att_ctx = attn_context(qk, z, p['sink_ctx'], dims) if ctx_out else None

        r, k, v, kk, a0, a1, lw0, lw1, g = rwkv_prep(z, p, dims)
        y0, y1 = rwkv_scan(r, k, v, kk, a0, a1, lw0, lw1, p['k_a'], dims)
        rw = rwkv_out(y0, y1, r, k, v, a0, a1, g, p, dims, row0=row0)

        cv = conformer_conv(z, p, dims, row0=row0)
        xa = out_proj(att_ctx, att_lat, rw, cv, w_out[l].astype(BF), xa, g1, dims, row0=row0)

        h2, logits = norm_mod(xa, norm2_g[l], sh2, sc2, dims, row0=row0, router_w=router_p)
        blk_expert, blk_active, slot_tok, wts, pos = moe_routing(logits, router_b)
        o_slots = moe_experts(h2, blk_expert, blk_active, slot_tok, moe_w_gate[l].astype(BF),
                              moe_w_up[l].astype(BF), moe_w_down[l].astype(BF))
        xa = moe_combine(o_slots, pos, wts, xa, g2, dims, row0=row0, drop_head_rows=not ctx_out)

    return xa.reshape(nb, s, d)
```

```python
import functools
import math
from typing import NamedTuple

import jax
import jax.numpy as jnp
from jax import lax
from jax.experimental import pallas as pl
from jax.experimental.pallas import tpu as pltpu

F32 = jnp.float32
BF = jnp.bfloat16

D_MODEL = 4096
DEPTH = 2
GRID_W = 64
ATT_HD = 128
ATT_HEADS = 12
KV_HEADS = 4
KV_GROUP = ATT_HEADS // KV_HEADS
ATT_BLOCK = 128
ROPE_BASE = 10000.0
ROPE_PAIRS = ATT_HD // 4
NEG_INF = -1e30
RWKV_HD = 64
RWKV_HEADS = 24
RWKV_WIDTH = RWKV_HEADS * RWKV_HD
DECAY_LORA = 64
ICLR_LORA = 64
GATE_LORA = 224
RWKV_GN_EPS = 64e-5
CONV_CH = 1024
CONV_K = 31
Q_COLS = ATT_HEADS * ATT_HD
KV_COLS = KV_HEADS * ATT_HD
ATT_COLS = Q_COLS + 2 * KV_COLS
RWKV_COLS = 3 * RWKV_WIDTH + 2 * DECAY_LORA + 2 * ICLR_LORA + GATE_LORA
CONV_COLS = 2 * CONV_CH
N_EXPERTS = 32
N_GROUPS = 4
EXPERTS_PER_GROUP = N_EXPERTS // N_GROUPS
TOP_K = 2
D_EXPERT = 640

LANES = 128
SUBLANES_BF16 = 16
VMEM_LIMIT = 56 * 1024 * 1024

RWKV_PAD = 5120
GATE_PAD = RWKV_PAD - (3 * RWKV_WIDTH + 2 * DECAY_LORA + 2 * ICLR_LORA)
Z_R, Z_K, Z_V = 0, RWKV_WIDTH, 2 * RWKV_WIDTH
Z_WD = 3 * RWKV_WIDTH
Z_AD = Z_WD + 2 * DECAY_LORA
Z_GD = Z_AD + 2 * ICLR_LORA
Z_CONV = RWKV_PAD
Z_ATT = RWKV_PAD + CONV_COLS
Z_COLS = Z_ATT + ATT_COLS

SCAN_CHUNK = 64
SCAN_GROUP_HEADS = 2
SCAN_GROUPS = 12
MOE_ROWS = 256
COMBINE_ROWS = 128


class Dims(NamedTuple):
    nb: int
    s: int
    cl: int

    @property
    def r_ctx(self):
        return self.nb * self.cl

    @property
    def r_lat(self):
        return self.nb * self.s

    @property
    def rows(self):
        return self.r_ctx + self.r_lat


def _cparams(n_axes):
    return pltpu.CompilerParams(dimension_semantics=("arbitrary",) * n_axes, vmem_limit_bytes=VMEM_LIMIT)


def _row_tile(pref, dims, within_seq=False):
    t = math.gcd(pref, math.gcd(dims.r_ctx, dims.s))
    if within_seq:
        t = math.gcd(t, dims.cl)
    return t


def _grp(i, tr, dims):
    nctx = dims.r_ctx // tr
    per = dims.s // tr
    return jnp.where(i < nctx, dims.nb, (i - nctx) // per)


def _seq_pos(i, tt, dims):
    nctx = dims.r_ctx // tt
    pc, pl_ = dims.cl // tt, dims.s // tt
    is_ctx = i < nctx
    pos = jnp.where(is_ctx, i % pc, (i - nctx) % pl_)
    cnt = jnp.where(is_ctx, pc, pl_)
    return pos, cnt


def _sigmoid(x):
    return 1.0 / (1.0 + jnp.exp(-x))


def _dot(a, b):
    return jnp.dot(a, b, preferred_element_type=F32)


def _dot_nt(a, b):
    return lax.dot_general(a, b, (((1,), (1,)), ((), ())), preferred_element_type=F32)


def _dot_tn(a, b):
    return lax.dot_general(a, b, (((0,), (0,)), ((), ())), preferred_element_type=F32)


def _split3(x):
    hi = x.astype(BF)
    r1 = x - hi.astype(F32)
    mid = r1.astype(BF)
    lo = (r1 - mid.astype(F32)).astype(BF)
    return hi, mid, lo


def _dot_exact_lhs(m_bf, x):
    hi, mid, lo = _split3(x)
    return _dot(m_bf, hi) + _dot(m_bf, mid) + _dot(m_bf, lo)


def _dot_exact_rhs(x, m_bf):
    hi, mid, lo = _split3(x)
    return _dot(hi, m_bf) + _dot(mid, m_bf) + _dot(lo, m_bf)


def _dot_hp(a, b):
    ah = a.astype(BF)
    al = (a - ah.astype(F32)).astype(BF)
    bh = b.astype(BF)
    bl = (b - bh.astype(F32)).astype(BF)
    return _dot(ah, bh) + (_dot(ah, bl) + _dot(al, bh))


def _seg_ones(width):
    i = lax.broadcasted_iota(jnp.int32, (LANES, LANES), 0) // width
    j = lax.broadcasted_iota(jnp.int32, (LANES, LANES), 1) // width
    return jnp.where(i == j, 1.0, 0.0).astype(BF)


def _seg_sum(x, seg):
    parts = [_dot_exact_rhs(x[:, j:j + LANES], seg) for j in range(0, x.shape[1], LANES)]
    return parts[0] if len(parts) == 1 else jnp.concatenate(parts, axis=1)


def _ada_kernel(c_ref, w_ref, b_ref, o_ref):
    cv = c_ref[...]
    act = (cv * _sigmoid(cv)).astype(BF)
    o_ref[0] = _dot(act, w_ref[0].astype(BF)) + b_ref[0]


def ada_tables(cs, ada_w, ada_b):
    n_layers, d, n = ada_w.shape
    tn = 512
    return pl.pallas_call(
        _ada_kernel,
        grid=(n_layers, n // tn),
        in_specs=[pl.BlockSpec((cs.shape[0], d), lambda l, j: (0, 0)),
                  pl.BlockSpec((1, d, tn), lambda l, j: (l, 0, j)),
                  pl.BlockSpec((1, 1, tn), lambda l, j: (l, 0, j))],
        out_specs=pl.BlockSpec((1, cs.shape[0], tn), lambda l, j: (l, 0, j)),
        out_shape=jax.ShapeDtypeStruct((n_layers, cs.shape[0], n), F32),
        compiler_params=_cparams(2),
        name="ada_tables",
    )(cs, ada_w, ada_b.reshape(n_layers, 1, n))


def _norm_mod_kernel(x_ref, g_ref, sh_ref, sc_ref, *rest, with_router):
    x = x_ref[...]
    ms = jnp.mean(x * x, axis=-1, keepdims=True)
    y = x * lax.rsqrt(ms + 1e-6) * g_ref[...]
    h = y * (1.0 + sc_ref[0]) + sh_ref[0]
    if with_router:
        rw_ref, h_ref, lg_ref = rest
        h_ref[...] = h
        lg_ref[...] = jnp.dot(h, rw_ref[...], precision=lax.Precision.HIGHEST, preferred_element_type=F32)
    else:
        (h_ref,) = rest
        h_ref[...] = h.astype(BF)


def norm_mod(x, g, sh, sc, dims, row0=0, router_w=None):
    d = x.shape[1]
    tr = _row_tile(256, dims)
    b0 = row0 // tr
    nblk = (x.shape[0] - row0) // tr
    in_specs = [pl.BlockSpec((tr, d), lambda i: (i + b0, 0)),
                pl.BlockSpec((1, d), lambda i: (0, 0)),
                pl.BlockSpec((1, 1, d), lambda i: (_grp(i + b0, tr, dims), 0, 0)),
                pl.BlockSpec((1, 1, d), lambda i: (_grp(i + b0, tr, dims), 0, 0))]
    args = [x, g.reshape(1, d), sh, sc]
    if router_w is None:
        out_specs = pl.BlockSpec((tr, d), lambda i: (i, 0))
        out_shape = jax.ShapeDtypeStruct((nblk * tr, d), BF)
    else:
        in_specs.append(pl.BlockSpec((d, LANES), lambda i: (0, 0)))
        args.append(router_w)
        out_specs = [pl.BlockSpec((tr, d), lambda i: (i, 0)), pl.BlockSpec((tr, LANES), lambda i: (i, 0))]
        out_shape = [jax.ShapeDtypeStruct((nblk * tr, d), F32), jax.ShapeDtypeStruct((nblk * tr, LANES), F32)]
    return pl.pallas_call(
        functools.partial(_norm_mod_kernel, with_router=router_w is not None),
        grid=(nblk,), in_specs=in_specs, out_specs=out_specs, out_shape=out_shape,
        compiler_params=_cparams(1),
        name="norm_mod_router" if router_w is not None else "norm_mod",
    )(*args)


def _mm_kernel(a_ref, b_ref, o_ref):
    o_ref[...] = _dot(a_ref[...], b_ref[...]).astype(o_ref.dtype)


def in_proj(h, w, dims):
    m, k = h.shape
    n = w.shape[1]
    tm = _row_tile(1024, dims)
    tn = 512
    return pl.pallas_call(
        _mm_kernel,
        grid=(m // tm, n // tn),
        in_specs=[pl.BlockSpec((tm, k), lambda i, j: (i, 0)),
                  pl.BlockSpec((k, tn), lambda i, j: (0, j))],
        out_specs=pl.BlockSpec((tm, tn), lambda i, j: (i, j)),
        out_shape=jax.ShapeDtypeStruct((m, n), BF),
        compiler_params=_cparams(2),
        name="in_proj",
    )(h, w)


def _qk_prep_kernel(za_ref, zb_ref, g_ref, cos_ref, sin_ref, o_ref):
    cos, sin = cos_ref[...], sin_ref[...]
    lane = lax.broadcasted_iota(jnp.int32, cos.shape, 1)
    first_half = (lane % (2 * ROPE_PAIRS)) < ROPE_PAIRS
    half_w = za_ref.shape[1]
    for h in range(o_ref.shape[1] // ATT_HD):
        c0 = h * ATT_HD
        src = za_ref if c0 < half_w else zb_ref
        x = src[:, c0 % half_w:c0 % half_w + ATT_HD].astype(F32)
        ms = jnp.mean(x * x, axis=-1, keepdims=True)
        y = x * lax.rsqrt(ms + 1e-6) * g_ref[:, c0:c0 + ATT_HD]
        partner = jnp.where(first_half, pltpu.roll(y, LANES - ROPE_PAIRS, 1), pltpu.roll(y, ROPE_PAIRS, 1))
        o_ref[:, c0:c0 + ATT_HD] = (y * cos + partner * sin).astype(BF)


def qk_prep(z, gains, cos_t, sin_t, dims):
    rows = z.shape[0]
    tr = _row_tile(256, dims, within_seq=True)
    qk_w = Q_COLS + KV_COLS
    half_w = qk_w // 2
    nctx = dims.r_ctx // tr
    pc, pl_ = dims.cl // tr, dims.s // tr

    def tab(i):
        return (jnp.where(i < nctx, i % pc, pc + (i - nctx) % pl_), 0)

    return pl.pallas_call(
        _qk_prep_kernel,
        grid=(rows // tr,),
        in_specs=[pl.BlockSpec((tr, half_w), lambda i: (i, Z_ATT // half_w)),
                  pl.BlockSpec((tr, half_w), lambda i: (i, Z_ATT // half_w + 1)),
                  pl.BlockSpec((1, qk_w), lambda i: (0, 0)),
                  pl.BlockSpec((tr, ATT_HD), tab),
                  pl.BlockSpec((tr, ATT_HD), tab)],
        out_specs=pl.BlockSpec((tr, qk_w), lambda i: (i, 0)),
        out_shape=jax.ShapeDtypeStruct((rows, qk_w), BF),
        compiler_params=_cparams(1),
        name="qk_prep",
    )(z, z, gains, cos_t, sin_t)


def _stack_heads(q):
    return jnp.concatenate([q[:, g * ATT_HD:(g + 1) * ATT_HD] for g in range(KV_GROUP)], axis=0)


def _attn_lat_kernel(q_ref, kp_ref, kc_ref, kn_ref, kx_ref, vp_ref, vc_ref, vn_ref, vx_ref, sink_ref, o_ref):
    n = pl.program_id(2)
    last = pl.num_programs(2) - 1
    q3 = _stack_heads(q_ref[...])
    s_p = _dot_nt(q3, kp_ref[...])
    s_c = _dot_nt(q3, kc_ref[...])
    s_n = _dot_nt(q3, kn_ref[...])
    s_x = _dot_nt(q3, kx_ref[...])
    i = lax.broadcasted_iota(jnp.int32, s_p.shape, 0) % ATT_BLOCK
    j = lax.broadcasted_iota(jnp.int32, s_p.shape, 1)
    s_p = jnp.where((j >= i) & (n > 0), s_p, NEG_INF)
    s_n = jnp.where((j <= i) & (n < last), s_n, NEG_INF)
    sk = sink_ref[0]
    m = jnp.maximum(jnp.maximum(jnp.max(s_p, axis=-1, keepdims=True), jnp.max(s_c, axis=-1, keepdims=True)),
                    jnp.maximum(jnp.max(s_n, axis=-1, keepdims=True), jnp.max(s_x, axis=-1, keepdims=True)))
    m = jnp.maximum(m, sk)
    p_p, p_c, p_n, p_x = jnp.exp(s_p - m), jnp.exp(s_c - m), jnp.exp(s_n - m), jnp.exp(s_x - m)
    den = (jnp.sum(p_p, axis=-1, keepdims=True) + jnp.sum(p_c, axis=-1, keepdims=True)
           + jnp.sum(p_n, axis=-1, keepdims=True) + jnp.sum(p_x, axis=-1, keepdims=True) + jnp.exp(sk - m))
    o = (_dot(p_p.astype(BF), vp_ref[...]) + _dot(p_c.astype(BF), vc_ref[...])
         + _dot(p_n.astype(BF), vn_ref[...]) + _dot(p_x.astype(BF), vx_ref[...])) / den
    for g in range(KV_GROUP):
        o_ref[:, g * ATT_HD:(g + 1) * ATT_HD] = o[g * ATT_BLOCK:(g + 1) * ATT_BLOCK].astype(BF)


def attn_latent(qk, z, sink_rows, dims):
    nblk = dims.s // ATT_BLOCK
    lat0 = dims.r_ctx // ATT_BLOCK
    kcol = Q_COLS // ATT_HD
    vcol = (Z_ATT + Q_COLS + KV_COLS) // ATT_HD

    def row(b, n):
        return lat0 + b * nblk + n

    def kv_specs(col0):
        return [pl.BlockSpec((ATT_BLOCK, ATT_HD), lambda b, g, n: (row(b, jnp.maximum(n - 1, 0)), col0 + g)),
                pl.BlockSpec((ATT_BLOCK, ATT_HD), lambda b, g, n: (row(b, n), col0 + g)),
                pl.BlockSpec((ATT_BLOCK, ATT_HD), lambda b, g, n: (row(b, jnp.minimum(n + 1, nblk - 1)), col0 + g)),
                pl.BlockSpec((dims.cl, ATT_HD), lambda b, g, n: (b, col0 + g))]

    gw = KV_GROUP * ATT_HD
    return pl.pallas_call(
        _attn_lat_kernel,
        grid=(dims.nb, KV_HEADS, nblk),
        in_specs=([pl.BlockSpec((ATT_BLOCK, gw), lambda b, g, n: (row(b, n), g))]
                  + kv_specs(kcol) + kv_specs(vcol)
                  + [pl.BlockSpec((1, KV_GROUP * ATT_BLOCK, 1), lambda b, g, n: (g, 0, 0))]),
        out_specs=pl.BlockSpec((ATT_BLOCK, gw), lambda b, g, n: (b * nblk + n, g)),
        out_shape=jax.ShapeDtypeStruct((dims.r_lat, Q_COLS), BF),
        compiler_params=_cparams(3),
        name="attn_latent",
    )(qk, qk, qk, qk, qk, z, z, z, z, sink_rows)


def _attn_ctx_kernel(q_ref, k_ref, v_ref, sink_ref, o_ref):
    q3 = _stack_heads(q_ref[...])
    s = _dot_nt(q3, k_ref[...])
    sk = sink_ref[0]
    m = jnp.maximum(jnp.max(s, axis=-1, keepdims=True), sk)
    p = jnp.exp(s - m)
    den = jnp.sum(p, axis=-1, keepdims=True) + jnp.exp(sk - m)
    o = _dot(p.astype(BF), v_ref[...]) / den
    cl = q_ref.shape[0]
    for g in range(KV_GROUP):
        o_ref[:, g * ATT_HD:(g + 1) * ATT_HD] = o[g * cl:(g + 1) * cl].astype(BF)


def attn_context(qk, z, sink_rows, dims):
    kcol = Q_COLS // ATT_HD
    vcol = (Z_ATT + Q_COLS + KV_COLS) // ATT_HD
    gw = KV_GROUP * ATT_HD
    return pl.pallas_call(
        _attn_ctx_kernel,
        grid=(dims.nb, KV_HEADS),
        in_specs=[pl.BlockSpec((dims.cl, gw), lambda b, g: (b, g)),
                  pl.BlockSpec((dims.cl, ATT_HD), lambda b, g: (b, kcol + g)),
                  pl.BlockSpec((dims.cl, ATT_HD), lambda b, g: (b, vcol + g)),
                  pl.BlockSpec((1, KV_GROUP * dims.cl, 1), lambda b, g: (g, 0, 0))],
        out_specs=pl.BlockSpec((dims.cl, gw), lambda b, g: (b, g)),
        out_shape=jax.ShapeDtypeStruct((dims.r_ctx, Q_COLS), BF),
        compiler_params=_cparams(2),
        name="attn_context",
    )(qk, z, z, sink_rows)


def _shift_rows(zf, prev_row, next_row):
    tt = zf.shape[0]
    row = lax.broadcasted_iota(jnp.int32, zf.shape, 0)
    prev = jnp.where(row == 0, prev_row, pltpu.roll(zf, 1, 0))
    nxt = jnp.where(row == tt - 1, next_row, pltpu.roll(zf, tt - 1, 0))
    return prev, nxt


def _rwkv_prep_kernel(z_ref, zp_ref, zn_ref, mup_ref, mun_ref, w0_ref, w2_ref, a0_ref, a2_ref, g2_ref, kk_ref,
                      r_o, k_o, v_o, kk_o, a0_o, a1_o, lw0_o, lw1_o, g_o, *, dims):
    i = pl.program_id(0)
    tt = z_ref.shape[0]
    pos, cnt = _seq_pos(i, tt, dims)
    zf = z_ref[...].astype(F32)
    hb = zp_ref.shape[0]
    prev_row = jnp.where(pos > 0, zp_ref[...].astype(F32)[hb - 1:hb, :], 0.0)
    next_row = jnp.where(pos < cnt - 1, zn_ref[...].astype(F32)[0:1, :], 0.0)
    prev, nxt = _shift_rows(zf, prev_row, next_row)
    zs = zf + (prev - zf) * mup_ref[...] + (nxt - zf) * mun_ref[...]
    w = RWKV_WIDTH
    r = zs[:, Z_R:Z_R + w]
    k = zs[:, Z_K:Z_K + w]
    v = zs[:, Z_V:Z_V + w]
    wd = zs[:, Z_WD:Z_WD + 2 * DECAY_LORA]
    ad = zs[:, Z_AD:Z_AD + 2 * ICLR_LORA]
    gd = zs[:, Z_GD:Z_GD + GATE_PAD]
    lw = -math.exp(-0.5) * _sigmoid(w0_ref[...] + _dot(jnp.tanh(wd).astype(BF), w2_ref[...]))
    a = _sigmoid(a0_ref[...] + _dot(ad.astype(BF), a2_ref[...]))
    g = _dot(_sigmoid(gd).astype(BF), g2_ref[...])
    kq = k * kk_ref[...]
    ssq = _seg_sum(kq * kq, _seg_ones(RWKV_HD))
    kk = kq * lax.rsqrt(jnp.maximum(ssq, 1e-24))
    r_o[...] = r.astype(BF)
    k_o[...] = k.astype(BF)
    v_o[...] = v.astype(BF)
    kk_o[...] = kk.astype(BF)
    a0_o[...] = a[:, :w].astype(BF)
    a1_o[...] = a[:, w:].astype(BF)
    lw0_o[...] = lw[:, :w]
    lw1_o[...] = lw[:, w:]
    g_o[...] = g.astype(BF)


def rwkv_prep(z, p, dims):
    rows = z.shape[0]
    tt = _row_tile(256, dims, within_seq=True)
    hb = SUBLANES_BF16
    nhb = rows // hb
    w = RWKV_WIDTH
    full = lambda shape: pl.BlockSpec(shape, lambda i: (0,) * len(shape))
    out_bf = jax.ShapeDtypeStruct((rows, w), BF)
    out_f = jax.ShapeDtypeStruct((rows, w), F32)
    ospec = pl.BlockSpec((tt, w), lambda i: (i, 0))
    return pl.pallas_call(
        functools.partial(_rwkv_prep_kernel, dims=dims),
        grid=(rows // tt,),
        in_specs=[pl.BlockSpec((tt, RWKV_PAD), lambda i: (i, 0)),
                  pl.BlockSpec((hb, RWKV_PAD), lambda i: (jnp.maximum(i * (tt // hb) - 1, 0), 0)),
                  pl.BlockSpec((hb, RWKV_PAD), lambda i: (jnp.minimum((i + 1) * (tt // hb), nhb - 1), 0)),
                  full((1, RWKV_PAD)), full((1, RWKV_PAD)),
                  full((1, 2 * w)), full((2 * DECAY_LORA, 2 * w)),
                  full((1, 2 * w)), full((2 * ICLR_LORA, 2 * w)),
                  full((GATE_PAD, w)), full((1, w))],
        out_specs=[ospec] * 9,
        out_shape=[out_bf, out_bf, out_bf, out_bf, out_bf, out_bf, out_f, out_f, out_bf],
        compiler_params=_cparams(1),
        name="rwkv_prep",
    )(z, z, z, p['mu_prev'], p['mu_next'], p['w0'], p['w2'], p['a0'], p['a2'], p['g2'], p['k_k'])


def _bmm(a, b):
    return lax.dot_general(a.astype(BF), b.astype(BF), (((2,), (1,)), ((0,), (0,))), preferred_element_type=F32)


def _bmm_nt(a, b):
    return lax.dot_general(a.astype(BF), b.astype(BF), (((2,), (2,)), ((0,), (0,))), preferred_element_type=F32)


def _bmm_nt_hp(a, b):
    ah = a.astype(BF)
    al = (a - ah.astype(F32)).astype(BF)
    bh = b.astype(BF)
    bl = (b - bh.astype(F32)).astype(BF)
    return _bmm_nt(ah, bh) + (_bmm_nt(ah, bl) + _bmm_nt(al, bh))


def _tri_inverse(lb):
    c = lb.shape[-1]
    row = lax.broadcasted_iota(jnp.int32, (c, c), 0)
    col = lax.broadcasted_iota(jnp.int32, (c, c), 1)
    joins = lambda m: ((row // (2 * m)) == (col // (2 * m))) & ((row // m) != (col // m))
    t = jnp.where(row == col, 1.0, 0.0) - jnp.where(joins(1), lb, 0.0)
    m = 2
    while m < c:
        t = t - _bmm(_bmm(t, jnp.where(joins(m), lb, 0.0)), t)
        m *= 2
    return t


def _head_stack(x, head_of_lane):
    heads = x.shape[1] // RWKV_HD
    return jnp.concatenate([jnp.where(head_of_lane == h, x, 0.0) for h in range(heads)], axis=0).astype(BF)


def _scan_group_operands(r, k, v, kk, a, lw, ka, s_bd, reverse):
    c, wdt = r.shape
    heads = wdt // RWKV_HD
    row = lax.broadcasted_iota(jnp.int32, (c, c), 0)
    col = lax.broadcasted_iota(jnp.int32, (c, c), 1)
    before = (col > row) if reverse else (col < row)
    upto = (col >= row) if reverse else (col <= row)
    kmod = k * (1.0 + (a - 1.0) * ka)
    b = kk * a
    cum = _dot_exact_lhs(jnp.where(upto, 1.0, 0.0).astype(BF), lw)
    tot = jnp.sum(lw, axis=0, keepdims=True)
    g_in, g_ex, g_inv, g_end = jnp.exp(cum), jnp.exp(cum - lw), jnp.exp(-cum), jnp.exp(tot - cum)
    kap = kk * g_ex
    r_t = r * g_in
    km_t = (kmod * g_inv).astype(BF)
    b_t = (b * g_inv).astype(BF)
    head_of_lane = lax.broadcasted_iota(jnp.int32, (c, wdt), 1) // RWKV_HD
    lhs = jnp.concatenate([_head_stack(kap, head_of_lane), _head_stack(r_t, head_of_lane)], axis=0)
    p = _dot_nt(lhs, jnp.concatenate([km_t, b_t], axis=0)).reshape(2 * heads, c, 2 * c)
    row2 = lax.broadcasted_iota(jnp.int32, (c, 2 * c), 0)
    col2 = lax.broadcasted_iota(jnp.int32, (c, 2 * c), 1) % c
    before2 = (col2 > row2) if reverse else (col2 < row2)
    upto2 = (col2 >= row2) if reverse else (col2 <= row2)
    lk_lb = jnp.where(before2, p[:heads], 0.0)
    kap_b = jnp.broadcast_to(kap.astype(BF)[None], (heads, c, wdt))
    r_b = jnp.broadcast_to(r_t.astype(BF)[None], (heads, c, wdt))
    return {
        'lb': lk_lb[:, :, c:],
        'x_rhs': jnp.concatenate([kap_b, lk_lb[:, :, :c].astype(BF)], axis=2),
        'y_rhs': jnp.concatenate([r_b, jnp.where(upto2, p[heads:], 0.0).astype(BF)], axis=2),
        's_rows': s_bd.astype(BF).reshape(heads, RWKV_HD, wdt),
        'vt': v.T.reshape(heads, RWKV_HD, c),
        'kb_e': jnp.concatenate([(kmod * g_end).astype(BF), (b * g_end).astype(BF)], axis=0),
        'g_tot': jnp.exp(tot),
    }


def _rwkv_scan_kernel(rf, kf, vf, kkf, af, lwf, rr, kr, vr, kkr, ar, lwr, ka_ref, y0_ref, y1_ref, s_ref):
    @pl.when(pl.program_id(2) == 0)
    def _():
        s_ref[...] = jnp.zeros(s_ref.shape, F32)

    gw = s_ref.shape[-1]
    heads = gw // RWKV_HD
    n_groups = s_ref.shape[1]
    c = rf.shape[0]
    dirs = ((rf, kf, vf, kkf, af, lwf, False), (rr, kr, vr, kkr, ar, lwr, True))
    groups = []
    for d, (r_ref, k_ref, v_ref, kk_ref, a_ref, lw_ref, reverse) in enumerate(dirs):
        for q in range(n_groups):
            sl = slice(q * gw, (q + 1) * gw)
            ld = lambda ref: ref[:, sl].astype(F32)
            s_bd = s_ref[d, q]
            ops = _scan_group_operands(ld(r_ref), ld(k_ref), ld(v_ref), ld(kk_ref), ld(a_ref), lw_ref[:, sl],
                                       ka_ref[:, sl], s_bd, reverse)
            ops['s_bd'] = s_bd
            groups.append(ops)

    cat = lambda key: jnp.concatenate([g[key] for g in groups], axis=0)
    lb, x_rhs, y_rhs, s_rows, vt = (cat(key) for key in ('lb', 'x_rhs', 'y_rhs', 's_rows', 'vt'))
    vtb = vt.astype(BF)
    xt = _bmm_nt(jnp.concatenate([s_rows, vtb], axis=2), x_rhs)
    t0 = _tri_inverse(lb)
    u0 = _bmm_nt(xt, t0)
    res = xt - u0 - _bmm_nt_hp(u0, lb)
    ut = u0 + _bmm_nt(res, t0)
    nutb = (-ut).astype(BF)
    yt = _bmm_nt(jnp.concatenate([s_rows, vtb, nutb], axis=2), y_rhs)

    row_head = lax.broadcasted_iota(jnp.int32, (gw, gw), 0) // RWKV_HD
    col_head = lax.broadcasted_iota(jnp.int32, (gw, gw), 1) // RWKV_HD
    for i, g in enumerate(groups):
        d, q = divmod(i, n_groups)
        sl = slice(q * gw, (q + 1) * gw)
        hs = slice(i * heads, (i + 1) * heads)
        y = yt[hs].reshape(gw, c).T
        (y0_ref, y1_ref)[d][:, sl] = y
        vu = jnp.concatenate([vtb[hs].reshape(gw, c), nutb[hs].reshape(gw, c)], axis=1)
        upd = g['s_bd'] * g['g_tot'] + _dot(vu, g['kb_e'])
        s_ref[d, q] = jnp.where(row_head == col_head, upd, 0.0)


def rwkv_scan(r, k, v, kk, a0, a1, lw0, lw1, k_a, dims):
    c = SCAN_CHUNK
    ncc, nlc = dims.cl // c, dims.s // c
    lat0 = dims.r_ctx // c
    gw = SCAN_GROUP_HEADS * RWKV_HD
    wdt = SCAN_GROUPS * gw

    def fwd(b, hb, t):
        return (jnp.where(t < ncc, b * ncc + t, lat0 + b * nlc + (t - ncc)), hb)

    def rev(b, hb, t):
        return (jnp.where(t < ncc, b * ncc + (ncc - 1 - t), lat0 + b * nlc + (nlc - 1 - (t - ncc))), hb)

    fs = pl.BlockSpec((c, wdt), fwd)
    rs = pl.BlockSpec((c, wdt), rev)
    y_shape = jax.ShapeDtypeStruct(lw0.shape, F32)
    return pl.pallas_call(
        _rwkv_scan_kernel,
        grid=(dims.nb, RWKV_WIDTH // wdt, ncc + nlc),
        in_specs=[fs] * 6 + [rs] * 6 + [pl.BlockSpec((1, wdt), lambda b, hb, t: (0, hb))],
        out_specs=[fs, rs],
        out_shape=[y_shape, y_shape],
        scratch_shapes=[pltpu.VMEM((2, SCAN_GROUPS, gw, gw), F32)],
        compiler_params=_cparams(3),
        name="rwkv_scan",
    )(r, k, v, kk, a0, lw0, r, k, v, kk, a1, lw1, k_a)


def _rwkv_out_kernel(y0_ref, y1_ref, r_ref, k_ref, v_ref, a0_ref, a1_ref, g_ref, ka_ref, rk_ref, lng_ref, lnb_ref,
                     o_ref):
    seg = _seg_ones(RWKV_HD)
    y = y0_ref[...] + y1_ref[...]
    mu = _seg_sum(y, seg) * (1.0 / RWKV_HD)
    yc = y - mu
    var = _seg_sum(yc * yc, seg) * (1.0 / RWKV_HD)
    yn = yc * lax.rsqrt(var + RWKV_GN_EPS) * lng_ref[...] + lnb_ref[...]
    ld = lambda ref: ref[...].astype(F32)
    r, k, v = ld(r_ref), ld(k_ref), ld(v_ref)
    kmod_sum = k * (2.0 + (ld(a0_ref) + ld(a1_ref) - 2.0) * ka_ref[...])
    bonus = _seg_sum(r * kmod_sum * rk_ref[...], seg) * v
    o_ref[...] = ((yn + bonus) * ld(g_ref)).astype(BF)


def rwkv_out(y0, y1, r, k, v, a0, a1, g, p, dims, row0=0):
    rows, w = y0.shape
    tr = _row_tile(256, dims)
    b0 = row0 // tr
    rs = pl.BlockSpec((tr, w), lambda i: (i + b0, 0))
    ps = pl.BlockSpec((1, w), lambda i: (0, 0))
    return pl.pallas_call(
        _rwkv_out_kernel,
        grid=((rows - row0) // tr,),
        in_specs=[rs] * 8 + [ps] * 4,
        out_specs=pl.BlockSpec((tr, w), lambda i: (i, 0)),
        out_shape=jax.ShapeDtypeStruct((rows - row0, w), BF),
        compiler_params=_cparams(1),
        name="rwkv_out",
    )(y0, y1, r, k, v, a0, a1, g, p['k_a'], p['r_k'], p['ln_g'], p['ln_b'])


def _conv_kernel(val_ref, gate_ref, vp_ref, gp_ref, vn_ref, gn_ref, w_ref, b_ref, lng_ref, lnb_ref, o_ref,
                 u_ref, acc_ref, *, dims, b0):
    i = pl.program_id(0) + b0
    tt = val_ref.shape[0]
    hb = vp_ref.shape[0]
    pos, cnt = _seq_pos(i, tt, dims)
    glu = lambda a, b: a.astype(F32) * _sigmoid(b.astype(F32))
    u_ref[0:hb, :] = jnp.where(pos > 0, glu(vp_ref[...], gp_ref[...]), 0.0)
    u_ref[hb:hb + tt, :] = glu(val_ref[...], gate_ref[...])
    u_ref[hb + tt:2 * hb + tt, :] = jnp.where(pos < cnt - 1, glu(vn_ref[...], gn_ref[...]), 0.0)
    half = CONV_K // 2
    for c0 in range(0, CONV_CH, LANES):
        acc = jnp.zeros((tt, LANES), F32) + b_ref[:, c0:c0 + LANES]
        for j in range(CONV_K):
            off = hb - half + j
            acc = acc + u_ref[off:off + tt, c0:c0 + LANES] * w_ref[j:j + 1, c0:c0 + LANES]
        acc_ref[:, c0:c0 + LANES] = acc
    y = acc_ref[...]
    mu = jnp.mean(y, axis=-1, keepdims=True)
    yc = y - mu
    var = jnp.mean(yc * yc, axis=-1, keepdims=True)
    yn = yc * lax.rsqrt(var + 1e-5) * lng_ref[...] + lnb_ref[...]
    o_ref[...] = (yn * _sigmoid(yn)).astype(BF)


def conformer_conv(z, p, dims, row0=0):
    rows = z.shape[0]
    tt = _row_tile(256, dims, within_seq=True)
    hb = SUBLANES_BF16
    nhb = rows // hb
    b0 = row0 // tt
    vcol, gcol = Z_CONV // CONV_CH, Z_CONV // CONV_CH + 1
    prev = lambda i: jnp.maximum((i + b0) * (tt // hb) - 1, 0)
    nxt = lambda i: jnp.minimum((i + b0 + 1) * (tt // hb), nhb - 1)
    ps = pl.BlockSpec((1, CONV_CH), lambda i: (0, 0))
    return pl.pallas_call(
        functools.partial(_conv_kernel, dims=dims, b0=b0),
        grid=((rows - row0) // tt,),
        in_specs=[pl.BlockSpec((tt, CONV_CH), lambda i: (i + b0, vcol)),
                  pl.BlockSpec((tt, CONV_CH), lambda i: (i + b0, gcol)),
                  pl.BlockSpec((hb, CONV_CH), lambda i: (prev(i), vcol)),
                  pl.BlockSpec((hb, CONV_CH), lambda i: (prev(i), gcol)),
                  pl.BlockSpec((hb, CONV_CH), lambda i: (nxt(i), vcol)),
                  pl.BlockSpec((hb, CONV_CH), lambda i: (nxt(i), gcol)),
                  pl.BlockSpec((CONV_K + 1, CONV_CH), lambda i: (0, 0)),
                  ps, ps, ps],
        out_specs=pl.BlockSpec((tt, CONV_CH), lambda i: (i, 0)),
        out_shape=jax.ShapeDtypeStruct((rows - row0, CONV_CH), BF),
        scratch_shapes=[pltpu.VMEM((tt + 2 * hb, CONV_CH), F32), pltpu.VMEM((tt, CONV_CH), F32)],
        compiler_params=_cparams(1),
        name="conformer_conv",
    )(z, z, z, z, z, z, p['conv_w'], p['conv_b'], p['conv_ln_g'], p['conv_ln_b'])


def _out_proj_kernel(*refs, n_ctx_blocks):
    if n_ctx_blocks:
        ac_ref, al_ref, r_ref, c_ref, wa_ref, wr_ref, wc_ref, x_ref, g_ref, o_ref = refs
        a = jnp.where(pl.program_id(0) < n_ctx_blocks, ac_ref[...], al_ref[...])
    else:
        al_ref, r_ref, c_ref, wa_ref, wr_ref, wc_ref, x_ref, g_ref, o_ref = refs
        a = al_ref[...]
    acc = _dot(a, wa_ref[...]) + _dot(r_ref[...], wr_ref[...]) + _dot(c_ref[...], wc_ref[...])
    o_ref[...] = x_ref[...] + g_ref[0] * acc


def out_proj(a_ctx, a_lat, rw, cv, w_out, x, gate, dims, row0=0):
    rows, d = x.shape
    tm = _row_tile(1024, dims)
    tn = 512
    b0 = row0 // tm
    nctx = dims.r_ctx // tm
    n_ctx_blocks = nctx - b0
    assert (a_ctx is not None) == (n_ctx_blocks > 0)
    a_specs, a_args = [], []
    if n_ctx_blocks:
        a_specs.append(pl.BlockSpec((tm, Q_COLS), lambda i, j: (jnp.minimum(i, nctx - 1), 0)))
        a_args.append(a_ctx)
    a_specs.append(pl.BlockSpec((tm, Q_COLS), lambda i, j: (jnp.maximum(i + b0 - nctx, 0), 0)))
    a_args.append(a_lat)
    return pl.pallas_call(
        functools.partial(_out_proj_kernel, n_ctx_blocks=n_ctx_blocks),
        grid=((rows - row0) // tm, d // tn),
        in_specs=a_specs + [
                  pl.BlockSpec((tm, RWKV_WIDTH), lambda i, j: (i, 0)),
                  pl.BlockSpec((tm, CONV_CH), lambda i, j: (i, 0)),
                  pl.BlockSpec((Q_COLS, tn), lambda i, j: (0, j)),
                  pl.BlockSpec((RWKV_WIDTH, tn), lambda i, j: (0, j)),
                  pl.BlockSpec((CONV_CH, tn), lambda i, j: (0, j)),
                  pl.BlockSpec((tm, tn), lambda i, j: (i + b0, j)),
                  pl.BlockSpec((1, 1, tn), lambda i, j: (_grp(i + b0, tm, dims), 0, j))],
        out_specs=pl.BlockSpec((tm, tn), lambda i, j: (i + b0, j)),
        out_shape=jax.ShapeDtypeStruct(x.shape, F32),
        input_output_aliases={len(a_args) + 5: 0},
        compiler_params=_cparams(2),
        name="out_proj",
    )(*a_args, rw, cv, w_out[:Q_COLS], w_out[Q_COLS:Q_COLS + RWKV_WIDTH], w_out[Q_COLS + RWKV_WIDTH:], x, gate)


def _row_gather(idx_ref, n, src_hbm, dst, sem):
    def body(r, carry):
        t = idx_ref[0, 0, r]
        pltpu.make_async_copy(src_hbm.at[pl.ds(t, 1)], dst.at[pl.ds(r, 1)], sem).start()
        return carry
    lax.fori_loop(0, n, body, 0, unroll=8)


def _moe_kernel(be_ref, act_ref, tok_ref, tokn_ref, h_hbm, wg_ref, wu_ref, wd_ref, o_ref, xbuf, sem):
    del be_ref
    i = pl.program_id(0)
    nblk = pl.num_programs(0)
    slot = i % 2
    rows = xbuf.shape[1]

    @pl.when(i == 0)
    def _():
        _row_gather(tok_ref, rows, h_hbm, xbuf.at[0], sem.at[0])

    @pl.when(i + 1 < nblk)
    def _():
        _row_gather(tokn_ref, rows, h_hbm, xbuf.at[1 - slot], sem.at[1 - slot])

    pltpu.make_async_copy(h_hbm.at[pl.ds(0, rows)], xbuf.at[slot], sem.at[slot]).wait()

    @pl.when(act_ref[i] > 0)
    def _():
        xb = xbuf[slot].astype(BF)
        gate = _dot(xb, wg_ref[0])
        up = _dot(xb, wu_ref[0])
        hid = (gate * _sigmoid(gate) * up).astype(BF)
        o_ref[...] = _dot(hid, wd_ref[0])

    @pl.when(act_ref[i] == 0)
    def _():
        o_ref[...] = jnp.zeros(o_ref.shape, F32)


def moe_experts(h2, blk_expert, blk_active, slot_tok, w_gate, w_up, w_down):
    n_blk = blk_expert.shape[0]
    d = h2.shape[1]
    de = w_gate.shape[2]
    b = MOE_ROWS
    tok3 = slot_tok.reshape(n_blk, 1, b)
    grid_spec = pltpu.PrefetchScalarGridSpec(
        num_scalar_prefetch=2,
        grid=(n_blk,),
        in_specs=[pl.BlockSpec((1, 1, b), lambda i, be, act: (i, 0, 0), memory_space=pltpu.SMEM),
                  pl.BlockSpec((1, 1, b), lambda i, be, act: (jnp.minimum(i + 1, n_blk - 1), 0, 0),
                               memory_space=pltpu.SMEM),
                  pl.BlockSpec(memory_space=pl.ANY),
                  pl.BlockSpec((1, d, de), lambda i, be, act: (be[i], 0, 0)),
                  pl.BlockSpec((1, d, de), lambda i, be, act: (be[i], 0, 0)),
                  pl.BlockSpec((1, de, d), lambda i, be, act: (be[i], 0, 0))],
        out_specs=pl.BlockSpec((b, d), lambda i, be, act: (i, 0)),
        scratch_shapes=[pltpu.VMEM((2, b, d), F32), pltpu.SemaphoreType.DMA((2,))],
    )
    return pl.pallas_call(
        _moe_kernel,
        grid_spec=grid_spec,
        out_shape=jax.ShapeDtypeStruct((n_blk * b, d), F32),
        compiler_params=_cparams(1),
        name="moe_experts",
    )(blk_expert, blk_active, tok3, tok3, h2, w_gate, w_up, w_down)


def _combine_kernel(pos_ref, posn_ref, o_hbm, wt_ref, x_ref, g_ref, out_ref, buf, sem):
    i = pl.program_id(0)
    nblk = pl.num_programs(0)
    slot = i % 2
    n2 = buf.shape[1]

    @pl.when(i == 0)
    def _():
        _row_gather(pos_ref, n2, o_hbm, buf.at[0], sem.at[0])

    @pl.when(i + 1 < nblk)
    def _():
        _row_gather(posn_ref, n2, o_hbm, buf.at[1 - slot], sem.at[1 - slot])

    pltpu.make_async_copy(o_hbm.at[pl.ds(0, n2)], buf.at[slot], sem.at[slot]).wait()
    tb = n2 // TOP_K
    wt = wt_ref[...]
    y = buf[slot, 0:tb, :] * wt[:, 0:1] + buf[slot, tb:n2, :] * wt[:, 1:2]
    out_ref[...] = x_ref[...] + g_ref[0] * y


def moe_combine(o_slots, pos, wts, x, gate, dims, row0=0, drop_head_rows=False):
    rows, d = x.shape
    tb = COMBINE_ROWS
    n_blk = (rows - row0) // tb
    b0 = row0 // tb
    out_row0 = row0 if drop_head_rows else 0
    ob0 = b0 - out_row0 // tb
    posb = pos.reshape(n_blk, tb, TOP_K).transpose(0, 2, 1).reshape(n_blk, 1, TOP_K * tb)
    return pl.pallas_call(
        _combine_kernel,
        grid=(n_blk,),
        in_specs=[pl.BlockSpec((1, 1, TOP_K * tb), lambda i: (i, 0, 0), memory_space=pltpu.SMEM),
                  pl.BlockSpec((1, 1, TOP_K * tb), lambda i: (jnp.minimum(i + 1, n_blk - 1), 0, 0),
                               memory_space=pltpu.SMEM),
                  pl.BlockSpec(memory_space=pl.ANY),
                  pl.BlockSpec((tb, TOP_K), lambda i: (i, 0)),
                  pl.BlockSpec((tb, d), lambda i: (i + b0, 0)),
                  pl.BlockSpec((1, 1, d), lambda i: (_grp(i + b0, tb, dims), 0, 0))],
        out_specs=pl.BlockSpec((tb, d), lambda i: (i + ob0, 0)),
        out_shape=jax.ShapeDtypeStruct((rows - out_row0, d), F32),
        scratch_shapes=[pltpu.VMEM((2, TOP_K * tb, d), F32), pltpu.SemaphoreType.DMA((2,))],
        input_output_aliases={} if drop_head_rows else {4: 0},
        compiler_params=_cparams(1),
        name="moe_combine",
    )(posb, posb, o_slots, wts, x, gate)


def _top2(v):
    assert TOP_K == 2
    idx = lax.broadcasted_iota(jnp.int32, v.shape, v.ndim - 1)
    i1 = jnp.argmax(v, axis=-1)
    rest = jnp.where(idx == i1[..., None], -jnp.inf, v)
    i2 = jnp.argmax(rest, axis=-1)
    vals = jnp.stack([jnp.max(v, axis=-1), jnp.max(rest, axis=-1)], axis=-1)
    return vals, jnp.stack([i1, i2], axis=-1).astype(jnp.int32)


def moe_routing(logits, router_b):
    n = logits.shape[0]
    scores = jax.nn.sigmoid(logits[:, :N_EXPERTS])
    sel = (scores + router_b.astype(F32)).reshape(n, N_GROUPS, EXPERTS_PER_GROUP)
    grp_score = jnp.sum(_top2(sel)[0], axis=-1)
    g_idx = jnp.argmax(grp_score, axis=-1)
    in_grp = jnp.take_along_axis(sel, g_idx[:, None, None], axis=1)[:, 0]
    _, local = _top2(in_grp)
    expert = g_idx[:, None] * EXPERTS_PER_GROUP + local
    wts = jnp.take_along_axis(scores, expert, axis=1)
    wts = wts / jnp.sum(wts, axis=-1, keepdims=True)

    b = MOE_ROWS
    nk = n * TOP_K
    e_flat = expert.reshape(-1).astype(jnp.int32)
    pos_flat, base = expert_slots(e_flat)
    pend = base[1:N_EXPERTS + 1]
    n_blocks = -(-(nk + N_EXPERTS * (b - 1)) // b)
    tok = jnp.arange(nk, dtype=jnp.int32) // TOP_K
    slot_tok = jnp.zeros((n_blocks * b,), jnp.int32).at[pos_flat].set(tok)
    blk_start = jnp.arange(n_blocks, dtype=jnp.int32) * b
    blk_expert = jnp.minimum(jnp.sum((pend[None, :] <= blk_start[:, None]).astype(jnp.int32), axis=1),
                             N_EXPERTS - 1)
    blk_active = (blk_start < pend[-1]).astype(jnp.int32)
    return blk_expert, blk_active, slot_tok, wts, pos_flat.reshape(n, TOP_K)


def _expert_slots_kernel(e_ref, pos_ref, base_ref, cnt_ref, *, block_rows):
    ph = pl.program_id(0)
    j = pl.program_id(1)
    e = e_ref[0]
    tb = e.shape[1]
    onehot = jnp.where(lax.broadcasted_iota(jnp.int32, (LANES, tb), 0) == e, 1.0, 0.0)
    in_block = jnp.sum(onehot, axis=1, keepdims=True)

    @pl.when((ph == 0) & (j == 0))
    def _():
        cnt_ref[...] = jnp.zeros(cnt_ref.shape, F32)

    @pl.when(ph == 0)
    def _():
        cnt_ref[...] += in_block

    @pl.when((ph == 1) & (j == 0))
    def _():
        padded = jnp.floor((cnt_ref[...] + (block_rows - 1)) * (1.0 / block_rows)) * block_rows
        row = lax.broadcasted_iota(jnp.int32, (LANES, LANES), 0)
        col = lax.broadcasted_iota(jnp.int32, (LANES, LANES), 1)
        earlier = jnp.where(col < row, 1.0, 0.0).astype(BF)
        base_ref[...] = _dot_exact_lhs(earlier, jnp.broadcast_to(padded, (LANES, LANES)))[:, 0:1]
        cnt_ref[...] = jnp.zeros(cnt_ref.shape, F32)

    @pl.when(ph == 1)
    def _():
        row = lax.broadcasted_iota(jnp.int32, (tb, tb), 0)
        col = lax.broadcasted_iota(jnp.int32, (tb, tb), 1)
        before = jnp.where(row < col, 1.0, 0.0).astype(BF)
        rank = _dot(onehot.astype(BF), before)
        slot = jnp.sum(onehot * (base_ref[...] + cnt_ref[...] + rank), axis=0, keepdims=True)
        pos_ref[0] = slot.astype(jnp.int32)
        cnt_ref[...] += in_block


def expert_slots(e_flat):
    nk = e_flat.shape[0]
    tb = math.gcd(nk, 512)
    n_blk = nk // tb
    pos, base = pl.pallas_call(
        functools.partial(_expert_slots_kernel, block_rows=MOE_ROWS),
        grid=(2, n_blk),
        in_specs=[pl.BlockSpec((1, 1, tb), lambda ph, j: (j, 0, 0))],
        out_specs=[pl.BlockSpec((1, 1, tb), lambda ph, j: (ph * j, 0, 0)),
                   pl.BlockSpec((LANES, 1), lambda ph, j: (0, 0))],
        out_shape=[jax.ShapeDtypeStruct((n_blk, 1, tb), jnp.int32), jax.ShapeDtypeStruct((LANES, 1), F32)],
        scratch_shapes=[pltpu.VMEM((LANES, 1), F32)],
        compiler_params=_cparams(2),
        name="expert_slots",
    )(e_flat.reshape(n_blk, 1, tb))
    return pos.reshape(nk), base[:, 0].astype(jnp.int32)


def _rope_tables(dims):
    rows = dims.s // GRID_W
    row = jnp.repeat(jnp.arange(rows), GRID_W).astype(F32)
    col = jnp.tile(jnp.arange(GRID_W), rows).astype(F32)
    inv = ROPE_BASE ** (-jnp.arange(ROPE_PAIRS, dtype=F32) / ROPE_PAIRS)
    ang = jnp.concatenate([row[:, None] * inv, row[:, None] * inv, col[:, None] * inv, col[:, None] * inv], axis=1)
    sign = jnp.tile(jnp.concatenate([-jnp.ones((ROPE_PAIRS,), F32), jnp.ones((ROPE_PAIRS,), F32)]), 2)
    cos_t = jnp.concatenate([jnp.ones((dims.cl, ATT_HD), F32), jnp.cos(ang)], axis=0)
    sin_t = jnp.concatenate([jnp.zeros((dims.cl, ATT_HD), F32), jnp.sin(ang) * sign], axis=0)
    return cos_t, sin_t


def _block_diag2(w):
    z = jnp.zeros_like(w[0])
    return jnp.concatenate([jnp.concatenate([w[0], z], axis=1), jnp.concatenate([z, w[1]], axis=1)], axis=0)


def _pad_cols(v, n):
    return jnp.pad(v, [(0, 0)] * (v.ndim - 1) + [(0, n - v.shape[-1])])


def _layer_params(l, w_in, q_norm_g, k_norm_g, attn_sink, rwkv_mu_prev, rwkv_mu_next, rwkv_w0, rwkv_w2, rwkv_a0,
                  rwkv_a2, rwkv_g2, rwkv_k_k, rwkv_k_a, rwkv_r_k, rwkv_ln_g, rwkv_ln_b, conv_w, conv_b, conv_ln_g,
                  conv_ln_b, dims):
    d = w_in.shape[1]
    wl = w_in[l]
    w_in_p = jnp.concatenate([wl[:, ATT_COLS:ATT_COLS + RWKV_COLS], jnp.zeros((d, RWKV_PAD - RWKV_COLS), F32),
                              wl[:, ATT_COLS + RWKV_COLS:], wl[:, :ATT_COLS]], axis=1).astype(BF)
    scale = ATT_HD ** -0.5
    gains = jnp.concatenate([jnp.tile(q_norm_g[l] * scale, (ATT_HEADS, 1)), jnp.tile(k_norm_g[l], (KV_HEADS, 1))],
                            axis=0).reshape(1, Q_COLS + KV_COLS)
    sink = attn_sink[l].astype(F32).reshape(KV_HEADS, KV_GROUP, 1)
    return {
        'w_in': w_in_p,
        'gains': gains,
        'sink_lat': jnp.repeat(sink, ATT_BLOCK, axis=1).reshape(KV_HEADS, KV_GROUP * ATT_BLOCK, 1),
        'sink_ctx': jnp.repeat(sink, dims.cl, axis=1).reshape(KV_HEADS, KV_GROUP * dims.cl, 1),
        'mu_prev': _pad_cols(rwkv_mu_prev[l][None], RWKV_PAD),
        'mu_next': _pad_cols(rwkv_mu_next[l][None], RWKV_PAD),
        'w0': rwkv_w0[l].reshape(1, 2 * RWKV_WIDTH),
        'w2': _block_diag2(rwkv_w2[l]).astype(BF),
        'a0': rwkv_a0[l].reshape(1, 2 * RWKV_WIDTH),
        'a2': _block_diag2(rwkv_a2[l]).astype(BF),
        'g2': jnp.pad(rwkv_g2[l], ((0, GATE_PAD - GATE_LORA), (0, 0))).astype(BF),
        'k_k': rwkv_k_k[l][None],
        'k_a': rwkv_k_a[l][None],
        'r_k': rwkv_r_k[l].reshape(1, RWKV_WIDTH),
        'ln_g': rwkv_ln_g[l][None],
        'ln_b': rwkv_ln_b[l][None],
        'conv_w': jnp.pad(conv_w[l], ((0, 1), (0, 0))),
        'conv_b': conv_b[l][None],
        'conv_ln_g': conv_ln_g[l][None],
        'conv_ln_b': conv_ln_b[l][None],
    }


def kernel(x, c, ctx, c_ctx, ada_w, ada_b, norm1_g, norm2_g, w_in, q_norm_g, k_norm_g, attn_sink, rwkv_mu_prev, rwkv_mu_next, rwkv_w0, rwkv_w2, rwkv_a0, rwkv_a2, rwkv_g2, rwkv_k_k, rwkv_k_a, rwkv_r_k, rwkv_ln_g, rwkv_ln_b, conv_w, conv_b, conv_ln_g, conv_ln_b, w_out, router_w, router_b, moe_w_gate, moe_w_up, moe_w_down):
    nb, s, d = x.shape
    dims = Dims(nb=nb, s=s, cl=ctx.shape[1])
    n_layers = w_in.shape[0]
    mod_rows = -(-(nb + 1) // 8) * 8
    cs = jnp.zeros((mod_rows, d), F32).at[:nb].set(c).at[nb].set(c_ctx)
    mods = ada_tables(cs, ada_w, ada_b).reshape(n_layers, mod_rows, 6, 1, d)
    cos_t, sin_t = _rope_tables(dims)
    router_p = _pad_cols(router_w.astype(F32), LANES)
    n_experts = moe_w_gate.shape[1]
    stack = lambda w: w.astype(BF).reshape((n_layers * n_experts,) + w.shape[2:])
    wg_all, wu_all, wd_all = stack(moe_w_gate), stack(moe_w_up), stack(moe_w_down)
    xa = jnp.concatenate([ctx.reshape(dims.r_ctx, d), x.reshape(dims.r_lat, d)], axis=0)

    for l in range(n_layers):
        ctx_out = l < n_layers - 1
        row0 = 0 if ctx_out else dims.r_ctx
        p = _layer_params(l, w_in, q_norm_g, k_norm_g, attn_sink, rwkv_mu_prev, rwkv_mu_next, rwkv_w0, rwkv_w2,
                          rwkv_a0, rwkv_a2, rwkv_g2, rwkv_k_k, rwkv_k_a, rwkv_r_k, rwkv_ln_g, rwkv_ln_b, conv_w,
                          conv_b, conv_ln_g, conv_ln_b, dims)
        sh1, sc1, g1, sh2, sc2, g2 = [mods[l, :, m] for m in range(6)]

        h1 = norm_mod(xa, norm1_g[l], sh1, sc1, dims)
        z = in_proj(h1, p['w_in'], dims)

        qk = qk_prep(z, p['gains'], cos_t, sin_t, dims)
        att_lat = attn_latent(qk, z, p['sink_lat'], dims)
        att_ctx = attn_context(qk, z, p['sink_ctx'], dims) if ctx_out else None

        r, k, v, kk, a0, a1, lw0, lw1, g = rwkv_prep(z, p, dims)
        y0, y1 = rwkv_scan(r, k, v, kk, a0, a1, lw0, lw1, p['k_a'], dims)
        rw = rwkv_out(y0, y1, r, k, v, a0, a1, g, p, dims, row0=row0)

        cv = conformer_conv(z, p, dims, row0=row0)
        xa = out_proj(att_ctx, att_lat, rw, cv, w_out[l].astype(BF), xa, g1, dims, row0=row0)

        h2, logits = norm_mod(xa, norm2_g[l], sh2, sc2, dims, row0=row0, router_w=router_p)
        blk_expert, blk_active, slot_tok, wts, pos = moe_routing(logits, router_b)
        o_slots = moe_experts(h2, blk_expert + l * n_experts, blk_active, slot_tok, wg_all, wu_all, wd_all)
        xa = moe_combine(o_slots, pos, wts, xa, g2, dims, row0=row0, drop_head_rows=not ctx_out)

    return xa.reshape(nb, s, d)
```

```python
import functools
import math
from typing import NamedTuple

import jax
import jax.numpy as jnp
from jax import lax
from jax.experimental import pallas as pl
from jax.experimental.pallas import tpu as pltpu

F32 = jnp.float32
BF = jnp.bfloat16

D_MODEL = 4096
DEPTH = 2
GRID_W = 64
ATT_HD = 128
ATT_HEADS = 12
KV_HEADS = 4
KV_GROUP = ATT_HEADS // KV_HEADS
ATT_BLOCK = 128
ROPE_BASE = 10000.0
ROPE_PAIRS = ATT_HD // 4
NEG_INF = -1e30
RWKV_HD = 64
RWKV_HEADS = 24
RWKV_WIDTH = RWKV_HEADS * RWKV_HD
DECAY_LORA = 64
ICLR_LORA = 64
GATE_LORA = 224
RWKV_GN_EPS = 64e-5
CONV_CH = 1024
CONV_K = 31
Q_COLS = ATT_HEADS * ATT_HD
KV_COLS = KV_HEADS * ATT_HD
ATT_COLS = Q_COLS + 2 * KV_COLS
RWKV_COLS = 3 * RWKV_WIDTH + 2 * DECAY_LORA + 2 * ICLR_LORA + GATE_LORA
CONV_COLS = 2 * CONV_CH
N_EXPERTS = 32
N_GROUPS = 4
EXPERTS_PER_GROUP = N_EXPERTS // N_GROUPS
TOP_K = 2
D_EXPERT = 640

LANES = 128
SUBLANES_BF16 = 16
VMEM_LIMIT = 56 * 1024 * 1024

RWKV_PAD = 5120
GATE_PAD = RWKV_PAD - (3 * RWKV_WIDTH + 2 * DECAY_LORA + 2 * ICLR_LORA)
Z_R, Z_K, Z_V = 0, RWKV_WIDTH, 2 * RWKV_WIDTH
Z_WD = 3 * RWKV_WIDTH
Z_AD = Z_WD + 2 * DECAY_LORA
Z_GD = Z_AD + 2 * ICLR_LORA
Z_CONV = RWKV_PAD
Z_ATT = RWKV_PAD + CONV_COLS
Z_COLS = Z_ATT + ATT_COLS

SCAN_CHUNK = 64
SCAN_GROUP_HEADS = 2
SCAN_GROUPS = 12
MOE_ROWS = 256
COMBINE_ROWS = 128


class Dims(NamedTuple):
    nb: int
    s: int
    cl: int

    @property
    def r_ctx(self):
        return self.nb * self.cl

    @property
    def r_lat(self):
        return self.nb * self.s

    @property
    def rows(self):
        return self.r_ctx + self.r_lat


def _cparams(n_axes):
    return pltpu.CompilerParams(dimension_semantics=("arbitrary",) * n_axes, vmem_limit_bytes=VMEM_LIMIT)


def _row_tile(pref, dims, within_seq=False):
    t = math.gcd(pref, math.gcd(dims.r_ctx, dims.s))
    if within_seq:
        t = math.gcd(t, dims.cl)
    return t


def _grp(i, tr, dims):
    nctx = dims.r_ctx // tr
    per = dims.s // tr
    return jnp.where(i < nctx, dims.nb, (i - nctx) // per)


def _seq_pos(i, tt, dims):
    nctx = dims.r_ctx // tt
    pc, pl_ = dims.cl // tt, dims.s // tt
    is_ctx = i < nctx
    pos = jnp.where(is_ctx, i % pc, (i - nctx) % pl_)
    cnt = jnp.where(is_ctx, pc, pl_)
    return pos, cnt


def _sigmoid(x):
    return 1.0 / (1.0 + jnp.exp(-x))


def _dot(a, b):
    return jnp.dot(a, b, preferred_element_type=F32)


def _dot_nt(a, b):
    return lax.dot_general(a, b, (((1,), (1,)), ((), ())), preferred_element_type=F32)


def _split3(x):
    hi = x.astype(BF)
    r1 = x - hi.astype(F32)
    mid = r1.astype(BF)
    lo = (r1 - mid.astype(F32)).astype(BF)
    return hi, mid, lo


def _dot_exact_lhs(m_bf, x):
    hi, mid, lo = _split3(x)
    return _dot(m_bf, hi) + _dot(m_bf, mid) + _dot(m_bf, lo)


def _dot_exact_rhs(x, m_bf):
    hi, mid, lo = _split3(x)
    return _dot(hi, m_bf) + _dot(mid, m_bf) + _dot(lo, m_bf)


def _seg_ones(width):
    i = lax.broadcasted_iota(jnp.int32, (LANES, LANES), 0) // width
    j = lax.broadcasted_iota(jnp.int32, (LANES, LANES), 1) // width
    return jnp.where(i == j, 1.0, 0.0).astype(BF)


def _seg_sum(x, seg):
    parts = [_dot_exact_rhs(x[:, j:j + LANES], seg) for j in range(0, x.shape[1], LANES)]
    return parts[0] if len(parts) == 1 else jnp.concatenate(parts, axis=1)


def _ada_kernel(c_ref, w_ref, b_ref, o_ref):
    cv = c_ref[...]
    act = (cv * _sigmoid(cv)).astype(BF)
    o_ref[0] = _dot(act, w_ref[0].astype(BF)) + b_ref[0]


def ada_tables(cs, ada_w, ada_b):
    n_layers, d, n = ada_w.shape
    tn = 512
    return pl.pallas_call(
        _ada_kernel,
        grid=(n_layers, n // tn),
        in_specs=[pl.BlockSpec((cs.shape[0], d), lambda l, j: (0, 0)),
                  pl.BlockSpec((1, d, tn), lambda l, j: (l, 0, j)),
                  pl.BlockSpec((1, 1, tn), lambda l, j: (l, 0, j))],
        out_specs=pl.BlockSpec((1, cs.shape[0], tn), lambda l, j: (l, 0, j)),
        out_shape=jax.ShapeDtypeStruct((n_layers, cs.shape[0], n), F32),
        compiler_params=_cparams(2),
        name="ada_tables",
    )(cs, ada_w, ada_b.reshape(n_layers, 1, n))


def _norm_mod_kernel(x_ref, g_ref, sh_ref, sc_ref, *rest, with_router):
    x = x_ref[...]
    ms = jnp.mean(x * x, axis=-1, keepdims=True)
    y = x * lax.rsqrt(ms + 1e-6) * g_ref[...]
    h = y * (1.0 + sc_ref[0]) + sh_ref[0]
    if with_router:
        rw_ref, h_ref, lg_ref = rest
        h_ref[...] = h
        lg_ref[...] = jnp.dot(h, rw_ref[...], precision=lax.Precision.HIGHEST, preferred_element_type=F32)
    else:
        (h_ref,) = rest
        h_ref[...] = h.astype(BF)


def norm_mod(x, g, sh, sc, dims, row0=0, router_w=None):
    d = x.shape[1]
    tr = _row_tile(256, dims)
    b0 = row0 // tr
    nblk = (x.shape[0] - row0) // tr
    in_specs = [pl.BlockSpec((tr, d), lambda i: (i + b0, 0)),
                pl.BlockSpec((1, d), lambda i: (0, 0)),
                pl.BlockSpec((1, 1, d), lambda i: (_grp(i + b0, tr, dims), 0, 0)),
                pl.BlockSpec((1, 1, d), lambda i: (_grp(i + b0, tr, dims), 0, 0))]
    args = [x, g.reshape(1, d), sh, sc]
    if router_w is None:
        out_specs = pl.BlockSpec((tr, d), lambda i: (i, 0))
        out_shape = jax.ShapeDtypeStruct((nblk * tr, d), BF)
    else:
        in_specs.append(pl.BlockSpec((d, LANES), lambda i: (0, 0)))
        args.append(router_w)
        out_specs = [pl.BlockSpec((tr, d), lambda i: (i, 0)), pl.BlockSpec((tr, LANES), lambda i: (i, 0))]
        out_shape = [jax.ShapeDtypeStruct((nblk * tr, d), F32), jax.ShapeDtypeStruct((nblk * tr, LANES), F32)]
    return pl.pallas_call(
        functools.partial(_norm_mod_kernel, with_router=router_w is not None),
        grid=(nblk,), in_specs=in_specs, out_specs=out_specs, out_shape=out_shape,
        compiler_params=_cparams(1),
        name="norm_mod_router" if router_w is not None else "norm_mod",
    )(*args)


def _mm_kernel(a_ref, b_ref, o_ref):
    o_ref[...] = _dot(a_ref[...], b_ref[...]).astype(o_ref.dtype)


def in_proj(h, w, dims):
    m, k = h.shape
    n = w.shape[1]
    tm = _row_tile(1024, dims)
    tn = 512
    return pl.pallas_call(
        _mm_kernel,
        grid=(m // tm, n // tn),
        in_specs=[pl.BlockSpec((tm, k), lambda i, j: (i, 0)),
                  pl.BlockSpec((k, tn), lambda i, j: (0, j))],
        out_specs=pl.BlockSpec((tm, tn), lambda i, j: (i, j)),
        out_shape=jax.ShapeDtypeStruct((m, n), BF),
        compiler_params=_cparams(2),
        name="in_proj",
    )(h, w)


def _qk_prep_kernel(za_ref, zb_ref, g_ref, cos_ref, sin_ref, o_ref):
    cos, sin = cos_ref[...], sin_ref[...]
    lane = lax.broadcasted_iota(jnp.int32, cos.shape, 1)
    first_half = (lane % (2 * ROPE_PAIRS)) < ROPE_PAIRS
    half_w = za_ref.shape[1]
    for h in range(o_ref.shape[1] // ATT_HD):
        c0 = h * ATT_HD
        src = za_ref if c0 < half_w else zb_ref
        x = src[:, c0 % half_w:c0 % half_w + ATT_HD].astype(F32)
        ms = jnp.mean(x * x, axis=-1, keepdims=True)
        y = x * lax.rsqrt(ms + 1e-6) * g_ref[:, c0:c0 + ATT_HD]
        partner = jnp.where(first_half, pltpu.roll(y, LANES - ROPE_PAIRS, 1), pltpu.roll(y, ROPE_PAIRS, 1))
        o_ref[:, c0:c0 + ATT_HD] = (y * cos + partner * sin).astype(BF)


def qk_prep(z, gains, cos_t, sin_t, dims):
    rows = z.shape[0]
    tr = _row_tile(256, dims, within_seq=True)
    qk_w = Q_COLS + KV_COLS
    half_w = qk_w // 2
    nctx = dims.r_ctx // tr
    pc, pl_ = dims.cl // tr, dims.s // tr

    def tab(i):
        return (jnp.where(i < nctx, i % pc, pc + (i - nctx) % pl_), 0)

    return pl.pallas_call(
        _qk_prep_kernel,
        grid=(rows // tr,),
        in_specs=[pl.BlockSpec((tr, half_w), lambda i: (i, Z_ATT // half_w)),
                  pl.BlockSpec((tr, half_w), lambda i: (i, Z_ATT // half_w + 1)),
                  pl.BlockSpec((1, qk_w), lambda i: (0, 0)),
                  pl.BlockSpec((tr, ATT_HD), tab),
                  pl.BlockSpec((tr, ATT_HD), tab)],
        out_specs=pl.BlockSpec((tr, qk_w), lambda i: (i, 0)),
        out_shape=jax.ShapeDtypeStruct((rows, qk_w), BF),
        compiler_params=_cparams(1),
        name="qk_prep",
    )(z, z, gains, cos_t, sin_t)


def _stack_heads(q):
    return jnp.concatenate([q[:, g * ATT_HD:(g + 1) * ATT_HD] for g in range(KV_GROUP)], axis=0)


def _attn_block(q, band_k, band_v, kx, vx, sk, prev_ok, next_ok):
    q3 = _stack_heads(q)
    s_p, s_c, s_n = (_dot_nt(q3, k) for k in band_k)
    s_x = _dot_nt(q3, kx)
    i = lax.broadcasted_iota(jnp.int32, s_p.shape, 0) % ATT_BLOCK
    j = lax.broadcasted_iota(jnp.int32, s_p.shape, 1)
    s_p = jnp.where((j >= i) & prev_ok, s_p, NEG_INF)
    s_n = jnp.where((j <= i) & next_ok, s_n, NEG_INF)
    lane_tiles = lambda t: [t[:, c0:c0 + LANES] for c0 in range(0, t.shape[1], LANES)]
    m = jnp.max(functools.reduce(jnp.maximum, [s_p, s_c, s_n] + lane_tiles(s_x)), axis=-1, keepdims=True)
    m = jnp.maximum(m, sk)
    p_p, p_c, p_n, p_x = jnp.exp(s_p - m), jnp.exp(s_c - m), jnp.exp(s_n - m), jnp.exp(s_x - m)
    den = (jnp.sum(functools.reduce(jnp.add, [p_p, p_c, p_n] + lane_tiles(p_x)), axis=-1, keepdims=True)
           + jnp.exp(sk - m))
    return (_dot(p_p.astype(BF), band_v[0]) + _dot(p_c.astype(BF), band_v[1])
            + _dot(p_n.astype(BF), band_v[2]) + _dot(p_x.astype(BF), vx)) / den


def _attn_lat_kernel(q_ref, kp_ref, kc_ref, kn_ref, kx_ref, vp_ref, vc_ref, vn_ref, vx_ref, sink_ref, o_ref):
    n = pl.program_id(2)
    last = pl.num_programs(2) - 1
    blk = ATT_BLOCK
    k_blocks = [kp_ref[...], kc_ref[0:blk, :], kc_ref[blk:2 * blk, :], kn_ref[...]]
    v_blocks = [vp_ref[...], vc_ref[0:blk, :], vc_ref[blk:2 * blk, :], vn_ref[...]]
    kx, vx, sk = kx_ref[...], vx_ref[...], sink_ref[0]
    for half in range(2):
        o = _attn_block(q_ref[half * blk:(half + 1) * blk, :], k_blocks[half:half + 3], v_blocks[half:half + 3],
                        kx, vx, sk, prev_ok=(n > 0) if half == 0 else True, next_ok=True if half == 0 else (n < last))
        for g in range(KV_GROUP):
            o_ref[half * blk:(half + 1) * blk, g * ATT_HD:(g + 1) * ATT_HD] = o[g * blk:(g + 1) * blk].astype(BF)


def attn_latent(qk, z, sink_rows, dims):
    nblk = dims.s // ATT_BLOCK
    npair = nblk // 2
    lat0 = dims.r_ctx // ATT_BLOCK
    kcol = Q_COLS // ATT_HD
    vcol = (Z_ATT + Q_COLS + KV_COLS) // ATT_HD

    def row(b, n):
        return lat0 + b * nblk + n

    def pair_row(b, n):
        return lat0 // 2 + b * npair + n

    def kv_specs(col0):
        return [pl.BlockSpec((ATT_BLOCK, ATT_HD), lambda b, g, n: (row(b, jnp.maximum(2 * n - 1, 0)), col0 + g)),
                pl.BlockSpec((2 * ATT_BLOCK, ATT_HD), lambda b, g, n: (pair_row(b, n), col0 + g)),
                pl.BlockSpec((ATT_BLOCK, ATT_HD), lambda b, g, n: (row(b, jnp.minimum(2 * n + 2, nblk - 1)), col0 + g)),
                pl.BlockSpec((dims.cl, ATT_HD), lambda b, g, n: (b, col0 + g))]

    gw = KV_GROUP * ATT_HD
    return pl.pallas_call(
        _attn_lat_kernel,
        grid=(dims.nb, KV_HEADS, npair),
        in_specs=([pl.BlockSpec((2 * ATT_BLOCK, gw), lambda b, g, n: (pair_row(b, n), g))]
                  + kv_specs(kcol) + kv_specs(vcol)
                  + [pl.BlockSpec((1, KV_GROUP * ATT_BLOCK, 1), lambda b, g, n: (g, 0, 0))]),
        out_specs=pl.BlockSpec((2 * ATT_BLOCK, gw), lambda b, g, n: (b * npair + n, g)),
        out_shape=jax.ShapeDtypeStruct((dims.r_lat, Q_COLS), BF),
        compiler_params=_cparams(3),
        name="attn_latent",
    )(qk, qk, qk, qk, qk, z, z, z, z, sink_rows)


def _attn_ctx_kernel(q_ref, k_ref, v_ref, sink_ref, o_ref):
    q3 = _stack_heads(q_ref[...])
    s = _dot_nt(q3, k_ref[...])
    sk = sink_ref[0]
    m = jnp.maximum(jnp.max(s, axis=-1, keepdims=True), sk)
    p = jnp.exp(s - m)
    den = jnp.sum(p, axis=-1, keepdims=True) + jnp.exp(sk - m)
    o = _dot(p.astype(BF), v_ref[...]) / den
    cl = q_ref.shape[0]
    for g in range(KV_GROUP):
        o_ref[:, g * ATT_HD:(g + 1) * ATT_HD] = o[g * cl:(g + 1) * cl].astype(BF)


def attn_context(qk, z, sink_rows, dims):
    kcol = Q_COLS // ATT_HD
    vcol = (Z_ATT + Q_COLS + KV_COLS) // ATT_HD
    gw = KV_GROUP * ATT_HD
    return pl.pallas_call(
        _attn_ctx_kernel,
        grid=(dims.nb, KV_HEADS),
        in_specs=[pl.BlockSpec((dims.cl, gw), lambda b, g: (b, g)),
                  pl.BlockSpec((dims.cl, ATT_HD), lambda b, g: (b, kcol + g)),
                  pl.BlockSpec((dims.cl, ATT_HD), lambda b, g: (b, vcol + g)),
                  pl.BlockSpec((1, KV_GROUP * dims.cl, 1), lambda b, g: (g, 0, 0))],
        out_specs=pl.BlockSpec((dims.cl, gw), lambda b, g: (b, g)),
        out_shape=jax.ShapeDtypeStruct((dims.r_ctx, Q_COLS), BF),
        compiler_params=_cparams(2),
        name="attn_context",
    )(qk, z, z, sink_rows)


def _shift_rows(zf, prev_row, next_row):
    tt = zf.shape[0]
    row = lax.broadcasted_iota(jnp.int32, zf.shape, 0)
    prev = jnp.where(row == 0, prev_row, pltpu.roll(zf, 1, 0))
    nxt = jnp.where(row == tt - 1, next_row, pltpu.roll(zf, tt - 1, 0))
    return prev, nxt


def _rwkv_prep_kernel(z_ref, zp_ref, zn_ref, mup_ref, mun_ref, w0_ref, w2_ref, a0_ref, a2_ref, g2_ref, kk_ref,
                      r_o, k_o, v_o, kk_o, a0_o, a1_o, lw0_o, lw1_o, g_o, *, dims):
    i = pl.program_id(0)
    tt = z_ref.shape[0]
    pos, cnt = _seq_pos(i, tt, dims)
    zf = z_ref[...].astype(F32)
    hb = zp_ref.shape[0]
    prev_row = jnp.where(pos > 0, zp_ref[...].astype(F32)[hb - 1:hb, :], 0.0)
    next_row = jnp.where(pos < cnt - 1, zn_ref[...].astype(F32)[0:1, :], 0.0)
    prev, nxt = _shift_rows(zf, prev_row, next_row)
    zs = zf + (prev - zf) * mup_ref[...] + (nxt - zf) * mun_ref[...]
    w = RWKV_WIDTH
    r = zs[:, Z_R:Z_R + w]
    k = zs[:, Z_K:Z_K + w]
    v = zs[:, Z_V:Z_V + w]
    wd = zs[:, Z_WD:Z_WD + 2 * DECAY_LORA]
    ad = zs[:, Z_AD:Z_AD + 2 * ICLR_LORA]
    gd = zs[:, Z_GD:Z_GD + GATE_PAD]
    lw = -math.exp(-0.5) * _sigmoid(w0_ref[...] + _dot(jnp.tanh(wd).astype(BF), w2_ref[...]))
    a = _sigmoid(a0_ref[...] + _dot(ad.astype(BF), a2_ref[...]))
    g = _dot(_sigmoid(gd).astype(BF), g2_ref[...])
    kq = k * kk_ref[...]
    ssq = _seg_sum(kq * kq, _seg_ones(RWKV_HD))
    kk = kq * lax.rsqrt(jnp.maximum(ssq, 1e-24))
    r_o[...] = r.astype(BF)
    k_o[...] = k.astype(BF)
    v_o[...] = v.astype(BF)
    kk_o[...] = kk.astype(BF)
    a0_o[...] = a[:, :w].astype(BF)
    a1_o[...] = a[:, w:].astype(BF)
    lw0_o[...] = lw[:, :w]
    lw1_o[...] = lw[:, w:]
    g_o[...] = g.astype(BF)


def rwkv_prep(z, p, dims):
    rows = z.shape[0]
    tt = _row_tile(256, dims, within_seq=True)
    hb = SUBLANES_BF16
    nhb = rows // hb
    w = RWKV_WIDTH
    full = lambda shape: pl.BlockSpec(shape, lambda i: (0,) * len(shape))
    out_bf = jax.ShapeDtypeStruct((rows, w), BF)
    out_f = jax.ShapeDtypeStruct((rows, w), F32)
    ospec = pl.BlockSpec((tt, w), lambda i: (i, 0))
    return pl.pallas_call(
        functools.partial(_rwkv_prep_kernel, dims=dims),
        grid=(rows // tt,),
        in_specs=[pl.BlockSpec((tt, RWKV_PAD), lambda i: (i, 0)),
                  pl.BlockSpec((hb, RWKV_PAD), lambda i: (jnp.maximum(i * (tt // hb) - 1, 0), 0)),
                  pl.BlockSpec((hb, RWKV_PAD), lambda i: (jnp.minimum((i + 1) * (tt // hb), nhb - 1), 0)),
                  full((1, RWKV_PAD)), full((1, RWKV_PAD)),
                  full((1, 2 * w)), full((2 * DECAY_LORA, 2 * w)),
                  full((1, 2 * w)), full((2 * ICLR_LORA, 2 * w)),
                  full((GATE_PAD, w)), full((1, w))],
        out_specs=[ospec] * 9,
        out_shape=[out_bf, out_bf, out_bf, out_bf, out_bf, out_bf, out_f, out_f, out_bf],
        compiler_params=_cparams(1),
        name="rwkv_prep",
    )(z, z, z, p['mu_prev'], p['mu_next'], p['w0'], p['w2'], p['a0'], p['a2'], p['g2'], p['k_k'])


def _bmm(a, b):
    return lax.dot_general(a.astype(BF), b.astype(BF), (((2,), (1,)), ((0,), (0,))), preferred_element_type=F32)


def _bmm_nt(a, b):
    return lax.dot_general(a.astype(BF), b.astype(BF), (((2,), (2,)), ((0,), (0,))), preferred_element_type=F32)


def _tri_inverse(lb):
    c = lb.shape[-1]
    row = lax.broadcasted_iota(jnp.int32, (c, c), 0)
    col = lax.broadcasted_iota(jnp.int32, (c, c), 1)
    joins = lambda m: ((row // (2 * m)) == (col // (2 * m))) & ((row // m) != (col // m))
    t = jnp.where(row == col, 1.0, 0.0) - jnp.where(joins(1), lb, 0.0)
    m = 2
    while m < c:
        t = t - _bmm(_bmm(t, jnp.where(joins(m), lb, 0.0)), t)
        m *= 2
    return t


def _head_stack(x, head_of_lane):
    heads = x.shape[1] // RWKV_HD
    return jnp.concatenate([jnp.where(head_of_lane == h, x, 0.0) for h in range(heads)], axis=0).astype(BF)


def _scan_group_operands(r, k, v, kk, a, lw, ka, s_bd, reverse):
    c, wdt = r.shape
    heads = wdt // RWKV_HD
    row = lax.broadcasted_iota(jnp.int32, (c, c), 0)
    col = lax.broadcasted_iota(jnp.int32, (c, c), 1)
    before = (col > row) if reverse else (col < row)
    upto = (col >= row) if reverse else (col <= row)
    kmod = k * (1.0 + (a - 1.0) * ka)
    b = kk * a
    cum = _dot_exact_lhs(jnp.where(upto, 1.0, 0.0).astype(BF), lw)
    tot = jnp.sum(lw, axis=0, keepdims=True)
    g_in, g_ex, g_inv, g_end = jnp.exp(cum), jnp.exp(cum - lw), jnp.exp(-cum), jnp.exp(tot - cum)
    kap = kk * g_ex
    r_t = r * g_in
    km_t = (kmod * g_inv).astype(BF)
    b_t = (b * g_inv).astype(BF)
    head_of_lane = lax.broadcasted_iota(jnp.int32, (c, wdt), 1) // RWKV_HD
    lhs = jnp.concatenate([_head_stack(kap, head_of_lane), _head_stack(r_t, head_of_lane)], axis=0)
    p = _dot_nt(lhs, jnp.concatenate([km_t, b_t], axis=0)).reshape(2 * heads, c, 2 * c)
    row2 = lax.broadcasted_iota(jnp.int32, (c, 2 * c), 0)
    col2 = lax.broadcasted_iota(jnp.int32, (c, 2 * c), 1) % c
    before2 = (col2 > row2) if reverse else (col2 < row2)
    upto2 = (col2 >= row2) if reverse else (col2 <= row2)
    lk_lb = jnp.where(before2, p[:heads], 0.0)
    kap_b = jnp.broadcast_to(kap.astype(BF)[None], (heads, c, wdt))
    r_b = jnp.broadcast_to(r_t.astype(BF)[None], (heads, c, wdt))
    return {
        'lb': lk_lb[:, :, c:],
        'x_rhs': jnp.concatenate([kap_b, lk_lb[:, :, :c].astype(BF)], axis=2),
        'y_rhs': jnp.concatenate([r_b, jnp.where(upto2, p[heads:], 0.0).astype(BF)], axis=2),
        's_rows': s_bd.astype(BF).reshape(heads, RWKV_HD, wdt),
        'vt': v.T.reshape(heads, RWKV_HD, c),
        'kb_e': jnp.concatenate([(kmod * g_end).astype(BF), (b * g_end).astype(BF)], axis=0),
        'g_tot': jnp.exp(tot),
    }


def _rwkv_scan_kernel(rf, kf, vf, kkf, af, lwf, rr, kr, vr, kkr, ar, lwr, ka_ref, y0_ref, y1_ref, s_ref):
    @pl.when(pl.program_id(2) == 0)
    def _():
        s_ref[...] = jnp.zeros(s_ref.shape, F32)

    gw = s_ref.shape[-1]
    heads = gw // RWKV_HD
    n_groups = s_ref.shape[1]
    c = rf.shape[0]
    dirs = ((rf, kf, vf, kkf, af, lwf, False), (rr, kr, vr, kkr, ar, lwr, True))
    groups = []
    for d, (r_ref, k_ref, v_ref, kk_ref, a_ref, lw_ref, reverse) in enumerate(dirs):
        for q in range(n_groups):
            sl = slice(q * gw, (q + 1) * gw)
            ld = lambda ref: ref[:, sl].astype(F32)
            s_bd = s_ref[d, q]
            ops = _scan_group_operands(ld(r_ref), ld(k_ref), ld(v_ref), ld(kk_ref), ld(a_ref), lw_ref[:, sl],
                                       ka_ref[:, sl], s_bd, reverse)
            ops['s_bd'] = s_bd
            groups.append(ops)

    cat = lambda key: jnp.concatenate([g[key] for g in groups], axis=0)
    lb, x_rhs, y_rhs, s_rows, vt = (cat(key) for key in ('lb', 'x_rhs', 'y_rhs', 's_rows', 'vt'))
    vtb = vt.astype(BF)
    xt = _bmm_nt(jnp.concatenate([s_rows, vtb], axis=2), x_rhs)
    t0 = _tri_inverse(lb)
    u0b = _bmm_nt(xt, t0).astype(BF)
    u0 = u0b.astype(F32)
    lb_hi = lb.astype(BF)
    lb_lo = (lb - lb_hi.astype(F32)).astype(BF)
    res = xt - u0 - (_bmm_nt(u0b, lb_hi) + _bmm_nt(u0b, lb_lo))
    ut = u0 + _bmm_nt(res, t0)
    nutb = (-ut).astype(BF)
    yt = _bmm_nt(jnp.concatenate([s_rows, vtb, nutb], axis=2), y_rhs)

    row_head = lax.broadcasted_iota(jnp.int32, (gw, gw), 0) // RWKV_HD
    col_head = lax.broadcasted_iota(jnp.int32, (gw, gw), 1) // RWKV_HD
    for i, g in enumerate(groups):
        d, q = divmod(i, n_groups)
        sl = slice(q * gw, (q + 1) * gw)
        hs = slice(i * heads, (i + 1) * heads)
        y = yt[hs].reshape(gw, c).T
        (y0_ref, y1_ref)[d][:, sl] = y
        vu = jnp.concatenate([vtb[hs].reshape(gw, c), nutb[hs].reshape(gw, c)], axis=1)
        upd = g['s_bd'] * g['g_tot'] + _dot(vu, g['kb_e'])
        s_ref[d, q] = jnp.where(row_head == col_head, upd, 0.0)


def rwkv_scan(r, k, v, kk, a0, a1, lw0, lw1, k_a, dims):
    c = SCAN_CHUNK
    ncc, nlc = dims.cl // c, dims.s // c
    lat0 = dims.r_ctx // c
    gw = SCAN_GROUP_HEADS * RWKV_HD
    wdt = SCAN_GROUPS * gw

    def fwd(b, hb, t):
        return (jnp.where(t < ncc, b * ncc + t, lat0 + b * nlc + (t - ncc)), hb)

    def rev(b, hb, t):
        return (jnp.where(t < ncc, b * ncc + (ncc - 1 - t), lat0 + b * nlc + (nlc - 1 - (t - ncc))), hb)

    fs = pl.BlockSpec((c, wdt), fwd)
    rs = pl.BlockSpec((c, wdt), rev)
    y_shape = jax.ShapeDtypeStruct(lw0.shape, F32)
    return pl.pallas_call(
        _rwkv_scan_kernel,
        grid=(dims.nb, RWKV_WIDTH // wdt, ncc + nlc),
        in_specs=[fs] * 6 + [rs] * 6 + [pl.BlockSpec((1, wdt), lambda b, hb, t: (0, hb))],
        out_specs=[fs, rs],
        out_shape=[y_shape, y_shape],
        scratch_shapes=[pltpu.VMEM((2, SCAN_GROUPS, gw, gw), F32)],
        compiler_params=_cparams(3),
        name="rwkv_scan",
    )(r, k, v, kk, a0, lw0, r, k, v, kk, a1, lw1, k_a)


def _rwkv_out_kernel(y0_ref, y1_ref, r_ref, k_ref, v_ref, a0_ref, a1_ref, g_ref, ka_ref, rk_ref, lng_ref, lnb_ref,
                     o_ref):
    seg = _seg_ones(RWKV_HD)
    y = y0_ref[...] + y1_ref[...]
    mu = _seg_sum(y, seg) * (1.0 / RWKV_HD)
    yc = y - mu
    var = _seg_sum(yc * yc, seg) * (1.0 / RWKV_HD)
    yn = yc * lax.rsqrt(var + RWKV_GN_EPS) * lng_ref[...] + lnb_ref[...]
    ld = lambda ref: ref[...].astype(F32)
    r, k, v = ld(r_ref), ld(k_ref), ld(v_ref)
    kmod_sum = k * (2.0 + (ld(a0_ref) + ld(a1_ref) - 2.0) * ka_ref[...])
    bonus = _seg_sum(r * kmod_sum * rk_ref[...], seg) * v
    o_ref[...] = ((yn + bonus) * ld(g_ref)).astype(BF)


def rwkv_out(y0, y1, r, k, v, a0, a1, g, p, dims, row0=0):
    rows, w = y0.shape
    tr = _row_tile(256, dims)
    b0 = row0 // tr
    rs = pl.BlockSpec((tr, w), lambda i: (i + b0, 0))
    ps = pl.BlockSpec((1, w), lambda i: (0, 0))
    return pl.pallas_call(
        _rwkv_out_kernel,
        grid=((rows - row0) // tr,),
        in_specs=[rs] * 8 + [ps] * 4,
        out_specs=pl.BlockSpec((tr, w), lambda i: (i, 0)),
        out_shape=jax.ShapeDtypeStruct((rows - row0, w), BF),
        compiler_params=_cparams(1),
        name="rwkv_out",
    )(y0, y1, r, k, v, a0, a1, g, p['k_a'], p['r_k'], p['ln_g'], p['ln_b'])


def _conv_kernel(val_ref, gate_ref, vp_ref, gp_ref, vn_ref, gn_ref, w_ref, b_ref, lng_ref, lnb_ref, o_ref,
                 u_ref, acc_ref, *, dims, b0):
    i = pl.program_id(0) + b0
    tt = val_ref.shape[0]
    hb = vp_ref.shape[0]
    pos, cnt = _seq_pos(i, tt, dims)
    glu = lambda a, b: a.astype(F32) * _sigmoid(b.astype(F32))
    u_ref[0:hb, :] = jnp.where(pos > 0, glu(vp_ref[...], gp_ref[...]), 0.0)
    u_ref[hb:hb + tt, :] = glu(val_ref[...], gate_ref[...])
    u_ref[hb + tt:2 * hb + tt, :] = jnp.where(pos < cnt - 1, glu(vn_ref[...], gn_ref[...]), 0.0)
    half = CONV_K // 2
    for c0 in range(0, CONV_CH, LANES):
        acc = jnp.zeros((tt, LANES), F32) + b_ref[:, c0:c0 + LANES]
        for j in range(CONV_K):
            off = hb - half + j
            acc = acc + u_ref[off:off + tt, c0:c0 + LANES] * w_ref[j:j + 1, c0:c0 + LANES]
        acc_ref[:, c0:c0 + LANES] = acc
    y = acc_ref[...]
    mu = jnp.mean(y, axis=-1, keepdims=True)
    yc = y - mu
    var = jnp.mean(yc * yc, axis=-1, keepdims=True)
    yn = yc * lax.rsqrt(var + 1e-5) * lng_ref[...] + lnb_ref[...]
    o_ref[...] = (yn * _sigmoid(yn)).astype(BF)


def conformer_conv(z, p, dims, row0=0):
    rows = z.shape[0]
    tt = _row_tile(256, dims, within_seq=True)
    hb = SUBLANES_BF16
    nhb = rows // hb
    b0 = row0 // tt
    vcol, gcol = Z_CONV // CONV_CH, Z_CONV // CONV_CH + 1
    prev = lambda i: jnp.maximum((i + b0) * (tt // hb) - 1, 0)
    nxt = lambda i: jnp.minimum((i + b0 + 1) * (tt // hb), nhb - 1)
    ps = pl.BlockSpec((1, CONV_CH), lambda i: (0, 0))
    return pl.pallas_call(
        functools.partial(_conv_kernel, dims=dims, b0=b0),
        grid=((rows - row0) // tt,),
        in_specs=[pl.BlockSpec((tt, CONV_CH), lambda i: (i + b0, vcol)),
                  pl.BlockSpec((tt, CONV_CH), lambda i: (i + b0, gcol)),
                  pl.BlockSpec((hb, CONV_CH), lambda i: (prev(i), vcol)),
                  pl.BlockSpec((hb, CONV_CH), lambda i: (prev(i), gcol)),
                  pl.BlockSpec((hb, CONV_CH), lambda i: (nxt(i), vcol)),
                  pl.BlockSpec((hb, CONV_CH), lambda i: (nxt(i), gcol)),
                  pl.BlockSpec((CONV_K + 1, CONV_CH), lambda i: (0, 0)),
                  ps, ps, ps],
        out_specs=pl.BlockSpec((tt, CONV_CH), lambda i: (i, 0)),
        out_shape=jax.ShapeDtypeStruct((rows - row0, CONV_CH), BF),
        scratch_shapes=[pltpu.VMEM((tt + 2 * hb, CONV_CH), F32), pltpu.VMEM((tt, CONV_CH), F32)],
        compiler_params=_cparams(1),
        name="conformer_conv",
    )(z, z, z, z, z, z, p['conv_w'], p['conv_b'], p['conv_ln_g'], p['conv_ln_b'])


def _out_proj_kernel(*refs, n_ctx_blocks):
    if n_ctx_blocks:
        ac_ref, al_ref, r_ref, c_ref, wa_ref, wr_ref, wc_ref, x_ref, g_ref, o_ref = refs
        a = jnp.where(pl.program_id(0) < n_ctx_blocks, ac_ref[...], al_ref[...])
    else:
        al_ref, r_ref, c_ref, wa_ref, wr_ref, wc_ref, x_ref, g_ref, o_ref = refs
        a = al_ref[...]
    acc = _dot(a, wa_ref[...]) + _dot(r_ref[...], wr_ref[...]) + _dot(c_ref[...], wc_ref[...])
    o_ref[...] = x_ref[...] + g_ref[0] * acc


def out_proj(a_ctx, a_lat, rw, cv, w_out, x, gate, dims, row0=0):
    rows, d = x.shape
    tm = _row_tile(1024, dims)
    tn = 512
    b0 = row0 // tm
    nctx = dims.r_ctx // tm
    n_ctx_blocks = nctx - b0
    assert (a_ctx is not None) == (n_ctx_blocks > 0)
    a_specs, a_args = [], []
    if n_ctx_blocks:
        a_specs.append(pl.BlockSpec((tm, Q_COLS), lambda i, j: (jnp.minimum(i, nctx - 1), 0)))
        a_args.append(a_ctx)
    a_specs.append(pl.BlockSpec((tm, Q_COLS), lambda i, j: (jnp.maximum(i + b0 - nctx, 0), 0)))
    a_args.append(a_lat)
    return pl.pallas_call(
        functools.partial(_out_proj_kernel, n_ctx_blocks=n_ctx_blocks),
        grid=((rows - row0) // tm, d // tn),
        in_specs=a_specs + [
                  pl.BlockSpec((tm, RWKV_WIDTH), lambda i, j: (i, 0)),
                  pl.BlockSpec((tm, CONV_CH), lambda i, j: (i, 0)),
                  pl.BlockSpec((Q_COLS, tn), lambda i, j: (0, j)),
                  pl.BlockSpec((RWKV_WIDTH, tn), lambda i, j: (0, j)),
                  pl.BlockSpec((CONV_CH, tn), lambda i, j: (0, j)),
                  pl.BlockSpec((tm, tn), lambda i, j: (i + b0, j)),
                  pl.BlockSpec((1, 1, tn), lambda i, j: (_grp(i + b0, tm, dims), 0, j))],
        out_specs=pl.BlockSpec((tm, tn), lambda i, j: (i + b0, j)),
        out_shape=jax.ShapeDtypeStruct(x.shape, F32),
        input_output_aliases={len(a_args) + 5: 0},
        compiler_params=_cparams(2),
        name="out_proj",
    )(*a_args, rw, cv, w_out[:Q_COLS], w_out[Q_COLS:Q_COLS + RWKV_WIDTH], w_out[Q_COLS + RWKV_WIDTH:], x, gate)


def _row_gather(idx_ref, n, src_hbm, dst, sem):
    def body(r, carry):
        t = idx_ref[0, 0, r]
        pltpu.make_async_copy(src_hbm.at[pl.ds(t, 1)], dst.at[pl.ds(r, 1)], sem).start()
        return carry
    lax.fori_loop(0, n, body, 0, unroll=8)


def _moe_kernel(be_ref, act_ref, tok_ref, tokn_ref, h_hbm, wg_ref, wu_ref, wd_ref, o_ref, xbuf, sem):
    del be_ref
    i = pl.program_id(0)
    nblk = pl.num_programs(0)
    slot = i % 2
    rows = xbuf.shape[1]

    @pl.when(i == 0)
    def _():
        _row_gather(tok_ref, rows, h_hbm, xbuf.at[0], sem.at[0])

    @pl.when(i + 1 < nblk)
    def _():
        _row_gather(tokn_ref, rows, h_hbm, xbuf.at[1 - slot], sem.at[1 - slot])

    pltpu.make_async_copy(h_hbm.at[pl.ds(0, rows)], xbuf.at[slot], sem.at[slot]).wait()

    @pl.when(act_ref[i] > 0)
    def _():
        xb = xbuf[slot].astype(BF)
        gate = _dot(xb, wg_ref[0])
        up = _dot(xb, wu_ref[0])
        hid = (gate * _sigmoid(gate) * up).astype(BF)
        o_ref[...] = _dot(hid, wd_ref[0])

    @pl.when(act_ref[i] == 0)
    def _():
        o_ref[...] = jnp.zeros(o_ref.shape, F32)


def moe_experts(h2, blk_expert, blk_active, slot_tok, w_gate, w_up, w_down):
    n_blk = blk_expert.shape[0]
    d = h2.shape[1]
    de = w_gate.shape[2]
    b = MOE_ROWS
    tok3 = slot_tok.reshape(n_blk, 1, b)
    grid_spec = pltpu.PrefetchScalarGridSpec(
        num_scalar_prefetch=2,
        grid=(n_blk,),
        in_specs=[pl.BlockSpec((1, 1, b), lambda i, be, act: (i, 0, 0), memory_space=pltpu.SMEM),
                  pl.BlockSpec((1, 1, b), lambda i, be, act: (jnp.minimum(i + 1, n_blk - 1), 0, 0),
                               memory_space=pltpu.SMEM),
                  pl.BlockSpec(memory_space=pl.ANY),
                  pl.BlockSpec((1, d, de), lambda i, be, act: (be[i], 0, 0)),
                  pl.BlockSpec((1, d, de), lambda i, be, act: (be[i], 0, 0)),
                  pl.BlockSpec((1, de, d), lambda i, be, act: (be[i], 0, 0))],
        out_specs=pl.BlockSpec((b, d), lambda i, be, act: (i, 0)),
        scratch_shapes=[pltpu.VMEM((2, b, d), F32), pltpu.SemaphoreType.DMA((2,))],
    )
    return pl.pallas_call(
        _moe_kernel,
        grid_spec=grid_spec,
        out_shape=jax.ShapeDtypeStruct((n_blk * b, d), F32),
        compiler_params=_cparams(1),
        name="moe_experts",
    )(blk_expert, blk_active, tok3, tok3, h2, w_gate, w_up, w_down)


def _combine_kernel(pos_ref, posn_ref, o_hbm, wt_ref, x_ref, g_ref, out_ref, buf, sem):
    i = pl.program_id(0)
    nblk = pl.num_programs(0)
    slot = i % 2
    n2 = buf.shape[1]

    @pl.when(i == 0)
    def _():
        _row_gather(pos_ref, n2, o_hbm, buf.at[0], sem.at[0])

    @pl.when(i + 1 < nblk)
    def _():
        _row_gather(posn_ref, n2, o_hbm, buf.at[1 - slot], sem.at[1 - slot])

    pltpu.make_async_copy(o_hbm.at[pl.ds(0, n2)], buf.at[slot], sem.at[slot]).wait()
    tb = n2 // TOP_K
    wt = wt_ref[...]
    y = buf[slot, 0:tb, :] * wt[:, 0:1] + buf[slot, tb:n2, :] * wt[:, 1:2]
    out_ref[...] = x_ref[...] + g_ref[0] * y


def moe_combine(o_slots, pos, wts, x, gate, dims, row0=0, drop_head_rows=False):
    rows, d = x.shape
    tb = COMBINE_ROWS
    n_blk = (rows - row0) // tb
    b0 = row0 // tb
    out_row0 = row0 if drop_head_rows else 0
    ob0 = b0 - out_row0 // tb
    posb = pos.reshape(n_blk, tb, TOP_K).transpose(0, 2, 1).reshape(n_blk, 1, TOP_K * tb)
    return pl.pallas_call(
        _combine_kernel,
        grid=(n_blk,),
        in_specs=[pl.BlockSpec((1, 1, TOP_K * tb), lambda i: (i, 0, 0), memory_space=pltpu.SMEM),
                  pl.BlockSpec((1, 1, TOP_K * tb), lambda i: (jnp.minimum(i + 1, n_blk - 1), 0, 0),
                               memory_space=pltpu.SMEM),
                  pl.BlockSpec(memory_space=pl.ANY),
                  pl.BlockSpec((tb, TOP_K), lambda i: (i, 0)),
                  pl.BlockSpec((tb, d), lambda i: (i + b0, 0)),
                  pl.BlockSpec((1, 1, d), lambda i: (_grp(i + b0, tb, dims), 0, 0))],
        out_specs=pl.BlockSpec((tb, d), lambda i: (i + ob0, 0)),
        out_shape=jax.ShapeDtypeStruct((rows - out_row0, d), F32),
        scratch_shapes=[pltpu.VMEM((2, TOP_K * tb, d), F32), pltpu.SemaphoreType.DMA((2,))],
        input_output_aliases={} if drop_head_rows else {4: 0},
        compiler_params=_cparams(1),
        name="moe_combine",
    )(posb, posb, o_slots, wts, x, gate)


def _top2(v):
    assert TOP_K == 2
    idx = lax.broadcasted_iota(jnp.int32, v.shape, v.ndim - 1)
    i1 = jnp.argmax(v, axis=-1)
    rest = jnp.where(idx == i1[..., None], -jnp.inf, v)
    i2 = jnp.argmax(rest, axis=-1)
    vals = jnp.stack([jnp.max(v, axis=-1), jnp.max(rest, axis=-1)], axis=-1)
    return vals, jnp.stack([i1, i2], axis=-1).astype(jnp.int32)


def moe_routing(logits, router_b):
    n = logits.shape[0]
    scores = jax.nn.sigmoid(logits[:, :N_EXPERTS])
    sel = (scores + router_b.astype(F32)).reshape(n, N_GROUPS, EXPERTS_PER_GROUP)
    grp_score = jnp.sum(_top2(sel)[0], axis=-1)
    g_idx = jnp.argmax(grp_score, axis=-1)
    in_grp = jnp.take_along_axis(sel, g_idx[:, None, None], axis=1)[:, 0]
    _, local = _top2(in_grp)
    expert = g_idx[:, None] * EXPERTS_PER_GROUP + local
    wts = jnp.take_along_axis(scores, expert, axis=1)
    wts = wts / jnp.sum(wts, axis=-1, keepdims=True)

    b = MOE_ROWS
    nk = n * TOP_K
    e_flat = expert.reshape(-1).astype(jnp.int32)
    pos_flat, base = expert_slots(e_flat)
    pend = base[1:N_EXPERTS + 1]
    n_blocks = -(-(nk + N_EXPERTS * (b - 1)) // b)
    tok = jnp.arange(nk, dtype=jnp.int32) // TOP_K
    slot_tok = jnp.zeros((n_blocks * b,), jnp.int32).at[pos_flat].set(tok)
    blk_start = jnp.arange(n_blocks, dtype=jnp.int32) * b
    blk_expert = jnp.minimum(jnp.sum((pend[None, :] <= blk_start[:, None]).astype(jnp.int32), axis=1),
                             N_EXPERTS - 1)
    blk_active = (blk_start < pend[-1]).astype(jnp.int32)
    return blk_expert, blk_active, slot_tok, wts, pos_flat.reshape(n, TOP_K)


def _expert_slots_kernel(e_ref, pos_ref, base_ref, cnt_ref, *, block_rows):
    ph = pl.program_id(0)
    j = pl.program_id(1)
    e = e_ref[0]
    tb = e.shape[1]
    onehot = jnp.where(lax.broadcasted_iota(jnp.int32, (LANES, tb), 0) == e, 1.0, 0.0)
    in_block = jnp.sum(onehot, axis=1, keepdims=True)

    @pl.when((ph == 0) & (j == 0))
    def _():
        cnt_ref[...] = jnp.zeros(cnt_ref.shape, F32)

    @pl.when(ph == 0)
    def _():
        cnt_ref[...] += in_block

    @pl.when((ph == 1) & (j == 0))
    def _():
        padded = jnp.floor((cnt_ref[...] + (block_rows - 1)) * (1.0 / block_rows)) * block_rows
        row = lax.broadcasted_iota(jnp.int32, (LANES, LANES), 0)
        col = lax.broadcasted_iota(jnp.int32, (LANES, LANES), 1)
        earlier = jnp.where(col < row, 1.0, 0.0).astype(BF)
        base_ref[...] = _dot_exact_lhs(earlier, jnp.broadcast_to(padded, (LANES, LANES)))[:, 0:1]
        cnt_ref[...] = jnp.zeros(cnt_ref.shape, F32)

    @pl.when(ph == 1)
    def _():
        row = lax.broadcasted_iota(jnp.int32, (tb, tb), 0)
        col = lax.broadcasted_iota(jnp.int32, (tb, tb), 1)
        before = jnp.where(row < col, 1.0, 0.0).astype(BF)
        rank = _dot(onehot.astype(BF), before)
        slot = jnp.sum(onehot * (base_ref[...] + cnt_ref[...] + rank), axis=0, keepdims=True)
        pos_ref[0] = slot.astype(jnp.int32)
        cnt_ref[...] += in_block


def expert_slots(e_flat):
    nk = e_flat.shape[0]
    tb = math.gcd(nk, 512)
    n_blk = nk // tb
    pos, base = pl.pallas_call(
        functools.partial(_expert_slots_kernel, block_rows=MOE_ROWS),
        grid=(2, n_blk),
        in_specs=[pl.BlockSpec((1, 1, tb), lambda ph, j: (j, 0, 0))],
        out_specs=[pl.BlockSpec((1, 1, tb), lambda ph, j: (ph * j, 0, 0)),
                   pl.BlockSpec((LANES, 1), lambda ph, j: (0, 0))],
        out_shape=[jax.ShapeDtypeStruct((n_blk, 1, tb), jnp.int32), jax.ShapeDtypeStruct((LANES, 1), F32)],
        scratch_shapes=[pltpu.VMEM((LANES, 1), F32)],
        compiler_params=_cparams(2),
        name="expert_slots",
    )(e_flat.reshape(n_blk, 1, tb))
    return pos.reshape(nk), base[:, 0].astype(jnp.int32)


def _rope_tables(dims):
    rows = dims.s // GRID_W
    row = jnp.repeat(jnp.arange(rows), GRID_W).astype(F32)
    col = jnp.tile(jnp.arange(GRID_W), rows).astype(F32)
    inv = ROPE_BASE ** (-jnp.arange(ROPE_PAIRS, dtype=F32) / ROPE_PAIRS)
    ang = jnp.concatenate([row[:, None] * inv, row[:, None] * inv, col[:, None] * inv, col[:, None] * inv], axis=1)
    sign = jnp.tile(jnp.concatenate([-jnp.ones((ROPE_PAIRS,), F32), jnp.ones((ROPE_PAIRS,), F32)]), 2)
    cos_t = jnp.concatenate([jnp.ones((dims.cl, ATT_HD), F32), jnp.cos(ang)], axis=0)
    sin_t = jnp.concatenate([jnp.zeros((dims.cl, ATT_HD), F32), jnp.sin(ang) * sign], axis=0)
    return cos_t, sin_t


def _block_diag2(w):
    z = jnp.zeros_like(w[0])
    return jnp.concatenate([jnp.concatenate([w[0], z], axis=1), jnp.concatenate([z, w[1]], axis=1)], axis=0)


def _pad_cols(v, n):
    return jnp.pad(v, [(0, 0)] * (v.ndim - 1) + [(0, n - v.shape[-1])])


def _layer_params(l, w_in, q_norm_g, k_norm_g, attn_sink, rwkv_mu_prev, rwkv_mu_next, rwkv_w0, rwkv_w2, rwkv_a0,
                  rwkv_a2, rwkv_g2, rwkv_k_k, rwkv_k_a, rwkv_r_k, rwkv_ln_g, rwkv_ln_b, conv_w, conv_b, conv_ln_g,
                  conv_ln_b, dims):
    d = w_in.shape[1]
    wl = w_in[l]
    w_in_p = jnp.concatenate([wl[:, ATT_COLS:ATT_COLS + RWKV_COLS], jnp.zeros((d, RWKV_PAD - RWKV_COLS), F32),
                              wl[:, ATT_COLS + RWKV_COLS:], wl[:, :ATT_COLS]], axis=1).astype(BF)
    scale = ATT_HD ** -0.5
    gains = jnp.concatenate([jnp.tile(q_norm_g[l] * scale, (ATT_HEADS, 1)), jnp.tile(k_norm_g[l], (KV_HEADS, 1))],
                            axis=0).reshape(1, Q_COLS + KV_COLS)
    sink = attn_sink[l].astype(F32).reshape(KV_HEADS, KV_GROUP, 1)
    return {
        'w_in': w_in_p,
        'gains': gains,
        'sink_lat': jnp.repeat(sink, ATT_BLOCK, axis=1).reshape(KV_HEADS, KV_GROUP * ATT_BLOCK, 1),
        'sink_ctx': jnp.repeat(sink, dims.cl, axis=1).reshape(KV_HEADS, KV_GROUP * dims.cl, 1),
        'mu_prev': _pad_cols(rwkv_mu_prev[l][None], RWKV_PAD),
        'mu_next': _pad_cols(rwkv_mu_next[l][None], RWKV_PAD),
        'w0': rwkv_w0[l].reshape(1, 2 * RWKV_WIDTH),
        'w2': _block_diag2(rwkv_w2[l]).astype(BF),
        'a0': rwkv_a0[l].reshape(1, 2 * RWKV_WIDTH),
        'a2': _block_diag2(rwkv_a2[l]).astype(BF),
        'g2': jnp.pad(rwkv_g2[l], ((0, GATE_PAD - GATE_LORA), (0, 0))).astype(BF),
        'k_k': rwkv_k_k[l][None],
        'k_a': rwkv_k_a[l][None],
        'r_k': rwkv_r_k[l].reshape(1, RWKV_WIDTH),
        'ln_g': rwkv_ln_g[l][None],
        'ln_b': rwkv_ln_b[l][None],
        'conv_w': jnp.pad(conv_w[l], ((0, 1), (0, 0))),
        'conv_b': conv_b[l][None],
        'conv_ln_g': conv_ln_g[l][None],
        'conv_ln_b': conv_ln_b[l][None],
    }


def kernel(x, c, ctx, c_ctx, ada_w, ada_b, norm1_g, norm2_g, w_in, q_norm_g, k_norm_g, attn_sink, rwkv_mu_prev, rwkv_mu_next, rwkv_w0, rwkv_w2, rwkv_a0, rwkv_a2, rwkv_g2, rwkv_k_k, rwkv_k_a, rwkv_r_k, rwkv_ln_g, rwkv_ln_b, conv_w, conv_b, conv_ln_g, conv_ln_b, w_out, router_w, router_b, moe_w_gate, moe_w_up, moe_w_down):
    nb, s, d = x.shape
    dims = Dims(nb=nb, s=s, cl=ctx.shape[1])
    n_layers = w_in.shape[0]
    mod_rows = -(-(nb + 1) // 8) * 8
    cs = jnp.zeros((mod_rows, d), F32).at[:nb].set(c).at[nb].set(c_ctx)
    mods = ada_tables(cs, ada_w, ada_b).reshape(n_layers, mod_rows, 6, 1, d)
    cos_t, sin_t = _rope_tables(dims)
    router_p = _pad_cols(router_w.astype(F32), LANES)
    n_experts = moe_w_gate.shape[1]
    stack = lambda w: w.astype(BF).reshape((n_layers * n_experts,) + w.shape[2:])
    wg_all, wu_all, wd_all = stack(moe_w_gate), stack(moe_w_up), stack(moe_w_down)
    xa = jnp.concatenate([ctx.reshape(dims.r_ctx, d), x.reshape(dims.r_lat, d)], axis=0)

    for l in range(n_layers):
        ctx_out = l < n_layers - 1
        row0 = 0 if ctx_out else dims.r_ctx
        p = _layer_params(l, w_in, q_norm_g, k_norm_g, attn_sink, rwkv_mu_prev, rwkv_mu_next, rwkv_w0, rwkv_w2,
                          rwkv_a0, rwkv_a2, rwkv_g2, rwkv_k_k, rwkv_k_a, rwkv_r_k, rwkv_ln_g, rwkv_ln_b, conv_w,
                          conv_b, conv_ln_g, conv_ln_b, dims)
        sh1, sc1, g1, sh2, sc2, g2 = [mods[l, :, m] for m in range(6)]

        h1 = norm_mod(xa, norm1_g[l], sh1, sc1, dims)
        z = in_proj(h1, p['w_in'], dims)

        qk = qk_prep(z, p['gains'], cos_t, sin_t, dims)
        att_lat = attn_latent(qk, z, p['sink_lat'], dims)
        att_ctx = attn_context(qk, z, p['sink_ctx'], dims) if ctx_out else None

        r, k, v, kk, a0, a1, lw0, lw1, g = rwkv_prep(z, p, dims)
        y0, y1 = rwkv_scan(r, k, v, kk, a0, a1, lw0, lw1, p['k_a'], dims)
        rw = rwkv_out(y0, y1, r, k, v, a0, a1, g, p, dims, row0=row0)

        cv = conformer_conv(z, p, dims, row0=row0)
        xa = out_proj(att_ctx, att_lat, rw, cv, w_out[l].astype(BF), xa, g1, dims, row0=row0)

        h2, logits = norm_mod(xa, norm2_g[l], sh2, sc2, dims, row0=row0, router_w=router_p)
        blk_expert, blk_active, slot_tok, wts, pos = moe_routing(logits, router_b)
        o_slots = moe_experts(h2, blk_expert + l * n_experts, blk_active, slot_tok, wg_all, wu_all, wd_all)
        xa = moe_combine(o_slots, pos, wts, xa, g2, dims, row0=row0, drop_head_rows=not ctx_out)

    return xa.reshape(nb, s, d)
```

```python
import functools
import math
from typing import NamedTuple

import jax
import jax.numpy as jnp
from jax import lax
from jax.experimental import pallas as pl
from jax.experimental.pallas import tpu as pltpu

F32 = jnp.float32
BF = jnp.bfloat16

D_MODEL = 4096
DEPTH = 2
GRID_W = 64
ATT_HD = 128
ATT_HEADS = 12
KV_HEADS = 4
KV_GROUP = ATT_HEADS // KV_HEADS
ATT_BLOCK = 128
ROPE_BASE = 10000.0
ROPE_PAIRS = ATT_HD // 4
NEG_INF = -1e30
RWKV_HD = 64
RWKV_HEADS = 24
RWKV_WIDTH = RWKV_HEADS * RWKV_HD
DECAY_LORA = 64
ICLR_LORA = 64
GATE_LORA = 224
RWKV_GN_EPS = 64e-5
CONV_CH = 1024
CONV_K = 31
Q_COLS = ATT_HEADS * ATT_HD
KV_COLS = KV_HEADS * ATT_HD
ATT_COLS = Q_COLS + 2 * KV_COLS
RWKV_COLS = 3 * RWKV_WIDTH + 2 * DECAY_LORA + 2 * ICLR_LORA + GATE_LORA
CONV_COLS = 2 * CONV_CH
N_EXPERTS = 32
N_GROUPS = 4
EXPERTS_PER_GROUP = N_EXPERTS // N_GROUPS
TOP_K = 2
D_EXPERT = 640

LANES = 128
SUBLANES_BF16 = 16
VMEM_LIMIT = 56 * 1024 * 1024

RWKV_PAD = 5120
GATE_PAD = RWKV_PAD - (3 * RWKV_WIDTH + 2 * DECAY_LORA + 2 * ICLR_LORA)
Z_R, Z_K, Z_V = 0, RWKV_WIDTH, 2 * RWKV_WIDTH
Z_WD = 3 * RWKV_WIDTH
Z_AD = Z_WD + 2 * DECAY_LORA
Z_GD = Z_AD + 2 * ICLR_LORA
Z_CONV = RWKV_PAD
Z_ATT = RWKV_PAD + CONV_COLS
Z_COLS = Z_ATT + ATT_COLS

SCAN_CHUNK = 64
SCAN_GROUP_HEADS = 2
SCAN_GROUPS = 12
MOE_ROWS = 256
COMBINE_ROWS = 128


class Dims(NamedTuple):
    nb: int
    s: int
    cl: int

    @property
    def r_ctx(self):
        return self.nb * self.cl

    @property
    def r_lat(self):
        return self.nb * self.s

    @property
    def rows(self):
        return self.r_ctx + self.r_lat


def _cparams(n_axes):
    return pltpu.CompilerParams(dimension_semantics=("arbitrary",) * n_axes, vmem_limit_bytes=VMEM_LIMIT)


def _row_tile(pref, dims, within_seq=False):
    t = math.gcd(pref, math.gcd(dims.r_ctx, dims.s))
    if within_seq:
        t = math.gcd(t, dims.cl)
    return t


def _grp(i, tr, dims):
    nctx = dims.r_ctx // tr
    per = dims.s // tr
    return jnp.where(i < nctx, dims.nb, (i - nctx) // per)


def _seq_pos(i, tt, dims):
    nctx = dims.r_ctx // tt
    pc, pl_ = dims.cl // tt, dims.s // tt
    is_ctx = i < nctx
    pos = jnp.where(is_ctx, i % pc, (i - nctx) % pl_)
    cnt = jnp.where(is_ctx, pc, pl_)
    return pos, cnt


def _sigmoid(x):
    return 1.0 / (1.0 + jnp.exp(-x))


def _dot(a, b):
    return jnp.dot(a, b, preferred_element_type=F32)


def _dot_nt(a, b):
    return lax.dot_general(a, b, (((1,), (1,)), ((), ())), preferred_element_type=F32)


def _split3(x):
    hi = x.astype(BF)
    r1 = x - hi.astype(F32)
    mid = r1.astype(BF)
    lo = (r1 - mid.astype(F32)).astype(BF)
    return hi, mid, lo


def _dot_exact_lhs(m_bf, x):
    hi, mid, lo = _split3(x)
    return _dot(m_bf, hi) + _dot(m_bf, mid) + _dot(m_bf, lo)


def _dot_exact_rhs(x, m_bf):
    hi, mid, lo = _split3(x)
    return _dot(hi, m_bf) + _dot(mid, m_bf) + _dot(lo, m_bf)


def _seg_ones(width):
    i = lax.broadcasted_iota(jnp.int32, (LANES, LANES), 0) // width
    j = lax.broadcasted_iota(jnp.int32, (LANES, LANES), 1) // width
    return jnp.where(i == j, 1.0, 0.0).astype(BF)


def _seg_sum(x, seg):
    parts = [_dot_exact_rhs(x[:, j:j + LANES], seg) for j in range(0, x.shape[1], LANES)]
    return parts[0] if len(parts) == 1 else jnp.concatenate(parts, axis=1)


def _ada_kernel(c_ref, w_ref, b_ref, o_ref):
    cv = c_ref[...]
    act = (cv * _sigmoid(cv)).astype(BF)
    o_ref[0] = _dot(act, w_ref[0].astype(BF)) + b_ref[0]


def ada_tables(cs, ada_w, ada_b):
    n_layers, d, n = ada_w.shape
    tn = 512
    return pl.pallas_call(
        _ada_kernel,
        grid=(n_layers, n // tn),
        in_specs=[pl.BlockSpec((cs.shape[0], d), lambda l, j: (0, 0)),
                  pl.BlockSpec((1, d, tn), lambda l, j: (l, 0, j)),
                  pl.BlockSpec((1, 1, tn), lambda l, j: (l, 0, j))],
        out_specs=pl.BlockSpec((1, cs.shape[0], tn), lambda l, j: (l, 0, j)),
        out_shape=jax.ShapeDtypeStruct((n_layers, cs.shape[0], n), F32),
        compiler_params=_cparams(2),
        name="ada_tables",
    )(cs, ada_w, ada_b.reshape(n_layers, 1, n))


def _norm_mod_kernel(x_ref, g_ref, sh_ref, sc_ref, *rest, with_router):
    x = x_ref[...]
    ms = jnp.mean(x * x, axis=-1, keepdims=True)
    y = x * lax.rsqrt(ms + 1e-6) * g_ref[...]
    h = y * (1.0 + sc_ref[0]) + sh_ref[0]
    if with_router:
        rw_ref, h_ref, lg_ref = rest
        h_ref[...] = h
        lg_ref[...] = jnp.dot(h, rw_ref[...], precision=lax.Precision.HIGHEST, preferred_element_type=F32)
    else:
        (h_ref,) = rest
        h_ref[...] = h.astype(BF)


def norm_mod(x, g, sh, sc, dims, row0=0, router_w=None):
    d = x.shape[1]
    tr = _row_tile(256, dims)
    b0 = row0 // tr
    nblk = (x.shape[0] - row0) // tr
    in_specs = [pl.BlockSpec((tr, d), lambda i: (i + b0, 0)),
                pl.BlockSpec((1, d), lambda i: (0, 0)),
                pl.BlockSpec((1, 1, d), lambda i: (_grp(i + b0, tr, dims), 0, 0)),
                pl.BlockSpec((1, 1, d), lambda i: (_grp(i + b0, tr, dims), 0, 0))]
    args = [x, g.reshape(1, d), sh, sc]
    if router_w is None:
        out_specs = pl.BlockSpec((tr, d), lambda i: (i, 0))
        out_shape = jax.ShapeDtypeStruct((nblk * tr, d), BF)
    else:
        in_specs.append(pl.BlockSpec((d, LANES), lambda i: (0, 0)))
        args.append(router_w)
        out_specs = [pl.BlockSpec((tr, d), lambda i: (i, 0)), pl.BlockSpec((tr, LANES), lambda i: (i, 0))]
        out_shape = [jax.ShapeDtypeStruct((nblk * tr, d), F32), jax.ShapeDtypeStruct((nblk * tr, LANES), F32)]
    return pl.pallas_call(
        functools.partial(_norm_mod_kernel, with_router=router_w is not None),
        grid=(nblk,), in_specs=in_specs, out_specs=out_specs, out_shape=out_shape,
        compiler_params=_cparams(1),
        name="norm_mod_router" if router_w is not None else "norm_mod",
    )(*args)


def _mm_kernel(a_ref, b_ref, o_ref):
    o_ref[...] = _dot(a_ref[...], b_ref[...]).astype(o_ref.dtype)


def in_proj(h, w, dims):
    m, k = h.shape
    n = w.shape[1]
    tm = _row_tile(1024, dims)
    tn = 512
    return pl.pallas_call(
        _mm_kernel,
        grid=(m // tm, n // tn),
        in_specs=[pl.BlockSpec((tm, k), lambda i, j: (i, 0)),
                  pl.BlockSpec((k, tn), lambda i, j: (0, j))],
        out_specs=pl.BlockSpec((tm, tn), lambda i, j: (i, j)),
        out_shape=jax.ShapeDtypeStruct((m, n), BF),
        compiler_params=_cparams(2),
        name="in_proj",
    )(h, w)


def _qk_prep_kernel(za_ref, zb_ref, g_ref, cos_ref, sin_ref, o_ref):
    cos, sin = cos_ref[...], sin_ref[...]
    lane = lax.broadcasted_iota(jnp.int32, cos.shape, 1)
    first_half = (lane % (2 * ROPE_PAIRS)) < ROPE_PAIRS
    half_w = za_ref.shape[1]
    for h in range(o_ref.shape[1] // ATT_HD):
        c0 = h * ATT_HD
        src = za_ref if c0 < half_w else zb_ref
        x = src[:, c0 % half_w:c0 % half_w + ATT_HD].astype(F32)
        ms = jnp.mean(x * x, axis=-1, keepdims=True)
        y = x * lax.rsqrt(ms + 1e-6) * g_ref[:, c0:c0 + ATT_HD]
        partner = jnp.where(first_half, pltpu.roll(y, LANES - ROPE_PAIRS, 1), pltpu.roll(y, ROPE_PAIRS, 1))
        o_ref[:, c0:c0 + ATT_HD] = (y * cos + partner * sin).astype(BF)


def qk_prep(z, gains, cos_t, sin_t, dims):
    rows = z.shape[0]
    tr = _row_tile(256, dims, within_seq=True)
    qk_w = Q_COLS + KV_COLS
    half_w = qk_w // 2
    nctx = dims.r_ctx // tr
    pc, pl_ = dims.cl // tr, dims.s // tr

    def tab(i):
        return (jnp.where(i < nctx, i % pc, pc + (i - nctx) % pl_), 0)

    return pl.pallas_call(
        _qk_prep_kernel,
        grid=(rows // tr,),
        in_specs=[pl.BlockSpec((tr, half_w), lambda i: (i, Z_ATT // half_w)),
                  pl.BlockSpec((tr, half_w), lambda i: (i, Z_ATT // half_w + 1)),
                  pl.BlockSpec((1, qk_w), lambda i: (0, 0)),
                  pl.BlockSpec((tr, ATT_HD), tab),
                  pl.BlockSpec((tr, ATT_HD), tab)],
        out_specs=pl.BlockSpec((tr, qk_w), lambda i: (i, 0)),
        out_shape=jax.ShapeDtypeStruct((rows, qk_w), BF),
        compiler_params=_cparams(1),
        name="qk_prep",
    )(z, z, gains, cos_t, sin_t)


def _stack_heads(q):
    return jnp.concatenate([q[:, g * ATT_HD:(g + 1) * ATT_HD] for g in range(KV_GROUP)], axis=0)


def _attn_block(q, band_k, band_v, kx, vx, sk, prev_ok, next_ok):
    q3 = _stack_heads(q)
    s_p, s_c, s_n = (_dot_nt(q3, k) for k in band_k)
    s_x = _dot_nt(q3, kx)
    i = lax.broadcasted_iota(jnp.int32, s_p.shape, 0) % ATT_BLOCK
    j = lax.broadcasted_iota(jnp.int32, s_p.shape, 1)
    s_p = jnp.where((j >= i) & prev_ok, s_p, NEG_INF)
    s_n = jnp.where((j <= i) & next_ok, s_n, NEG_INF)
    lane_tiles = lambda t: [t[:, c0:c0 + LANES] for c0 in range(0, t.shape[1], LANES)]
    m = jnp.max(functools.reduce(jnp.maximum, [s_p, s_c, s_n] + lane_tiles(s_x)), axis=-1, keepdims=True)
    m = jnp.maximum(m, sk)
    p_p, p_c, p_n, p_x = jnp.exp(s_p - m), jnp.exp(s_c - m), jnp.exp(s_n - m), jnp.exp(s_x - m)
    den = (jnp.sum(functools.reduce(jnp.add, [p_p, p_c, p_n] + lane_tiles(p_x)), axis=-1, keepdims=True)
           + jnp.exp(sk - m))
    return (_dot(p_p.astype(BF), band_v[0]) + _dot(p_c.astype(BF), band_v[1])
            + _dot(p_n.astype(BF), band_v[2]) + _dot(p_x.astype(BF), vx)) / den


def _attn_lat_kernel(q_ref, kp_ref, kc_ref, kn_ref, kx_ref, vp_ref, vc_ref, vn_ref, vx_ref, sink_ref, o_ref):
    n = pl.program_id(2)
    last = pl.num_programs(2) - 1
    blk = ATT_BLOCK
    k_blocks = [kp_ref[...], kc_ref[0:blk, :], kc_ref[blk:2 * blk, :], kn_ref[...]]
    v_blocks = [vp_ref[...], vc_ref[0:blk, :], vc_ref[blk:2 * blk, :], vn_ref[...]]
    kx, vx, sk = kx_ref[...], vx_ref[...], sink_ref[0]
    for half in range(2):
        o = _attn_block(q_ref[half * blk:(half + 1) * blk, :], k_blocks[half:half + 3], v_blocks[half:half + 3],
                        kx, vx, sk, prev_ok=(n > 0) if half == 0 else True, next_ok=True if half == 0 else (n < last))
        for g in range(KV_GROUP):
            o_ref[half * blk:(half + 1) * blk, g * ATT_HD:(g + 1) * ATT_HD] = o[g * blk:(g + 1) * blk].astype(BF)


def attn_latent(qk, z, sink_rows, dims):
    nblk = dims.s // ATT_BLOCK
    npair = nblk // 2
    lat0 = dims.r_ctx // ATT_BLOCK
    kcol = Q_COLS // ATT_HD
    vcol = (Z_ATT + Q_COLS + KV_COLS) // ATT_HD

    def row(b, n):
        return lat0 + b * nblk + n

    def pair_row(b, n):
        return lat0 // 2 + b * npair + n

    def kv_specs(col0):
        return [pl.BlockSpec((ATT_BLOCK, ATT_HD), lambda b, g, n: (row(b, jnp.maximum(2 * n - 1, 0)), col0 + g)),
                pl.BlockSpec((2 * ATT_BLOCK, ATT_HD), lambda b, g, n: (pair_row(b, n), col0 + g)),
                pl.BlockSpec((ATT_BLOCK, ATT_HD), lambda b, g, n: (row(b, jnp.minimum(2 * n + 2, nblk - 1)), col0 + g)),
                pl.BlockSpec((dims.cl, ATT_HD), lambda b, g, n: (b, col0 + g))]

    gw = KV_GROUP * ATT_HD
    return pl.pallas_call(
        _attn_lat_kernel,
        grid=(dims.nb, KV_HEADS, npair),
        in_specs=([pl.BlockSpec((2 * ATT_BLOCK, gw), lambda b, g, n: (pair_row(b, n), g))]
                  + kv_specs(kcol) + kv_specs(vcol)
                  + [pl.BlockSpec((1, KV_GROUP * ATT_BLOCK, 1), lambda b, g, n: (g, 0, 0))]),
        out_specs=pl.BlockSpec((2 * ATT_BLOCK, gw), lambda b, g, n: (b * npair + n, g)),
        out_shape=jax.ShapeDtypeStruct((dims.r_lat, Q_COLS), BF),
        compiler_params=_cparams(3),
        name="attn_latent",
    )(qk, qk, qk, qk, qk, z, z, z, z, sink_rows)


def _attn_ctx_kernel(q_ref, k_ref, v_ref, sink_ref, o_ref):
    q3 = _stack_heads(q_ref[...])
    s = _dot_nt(q3, k_ref[...])
    sk = sink_ref[0]
    m = jnp.maximum(jnp.max(s, axis=-1, keepdims=True), sk)
    p = jnp.exp(s - m)
    den = jnp.sum(p, axis=-1, keepdims=True) + jnp.exp(sk - m)
    o = _dot(p.astype(BF), v_ref[...]) / den
    cl = q_ref.shape[0]
    for g in range(KV_GROUP):
        o_ref[:, g * ATT_HD:(g + 1) * ATT_HD] = o[g * cl:(g + 1) * cl].astype(BF)


def attn_context(qk, z, sink_rows, dims):
    kcol = Q_COLS // ATT_HD
    vcol = (Z_ATT + Q_COLS + KV_COLS) // ATT_HD
    gw = KV_GROUP * ATT_HD
    return pl.pallas_call(
        _attn_ctx_kernel,
        grid=(dims.nb, KV_HEADS),
        in_specs=[pl.BlockSpec((dims.cl, gw), lambda b, g: (b, g)),
                  pl.BlockSpec((dims.cl, ATT_HD), lambda b, g: (b, kcol + g)),
                  pl.BlockSpec((dims.cl, ATT_HD), lambda b, g: (b, vcol + g)),
                  pl.BlockSpec((1, KV_GROUP * dims.cl, 1), lambda b, g: (g, 0, 0))],
        out_specs=pl.BlockSpec((dims.cl, gw), lambda b, g: (b, g)),
        out_shape=jax.ShapeDtypeStruct((dims.r_ctx, Q_COLS), BF),
        compiler_params=_cparams(2),
        name="attn_context",
    )(qk, z, z, sink_rows)


def _shift_rows(zf, prev_row, next_row):
    tt = zf.shape[0]
    row = lax.broadcasted_iota(jnp.int32, zf.shape, 0)
    prev = jnp.where(row == 0, prev_row, pltpu.roll(zf, 1, 0))
    nxt = jnp.where(row == tt - 1, next_row, pltpu.roll(zf, tt - 1, 0))
    return prev, nxt


def _rwkv_prep_kernel(z_ref, zp_ref, zn_ref, mup_ref, mun_ref, w0_ref, w2_ref, a0_ref, a2_ref, g2_ref, kk_ref,
                      r_o, k_o, v_o, kk_o, a0_o, a1_o, lw0_o, lw1_o, g_o, *, dims):
    i = pl.program_id(0)
    tt = z_ref.shape[0]
    pos, cnt = _seq_pos(i, tt, dims)
    zf = z_ref[...].astype(F32)
    hb = zp_ref.shape[0]
    prev_row = jnp.where(pos > 0, zp_ref[...].astype(F32)[hb - 1:hb, :], 0.0)
    next_row = jnp.where(pos < cnt - 1, zn_ref[...].astype(F32)[0:1, :], 0.0)
    prev, nxt = _shift_rows(zf, prev_row, next_row)
    zs = zf + (prev - zf) * mup_ref[...] + (nxt - zf) * mun_ref[...]
    w = RWKV_WIDTH
    r = zs[:, Z_R:Z_R + w]
    k = zs[:, Z_K:Z_K + w]
    v = zs[:, Z_V:Z_V + w]
    wd = zs[:, Z_WD:Z_WD + 2 * DECAY_LORA]
    ad = zs[:, Z_AD:Z_AD + 2 * ICLR_LORA]
    gd = zs[:, Z_GD:Z_GD + GATE_PAD]
    lw = -math.exp(-0.5) * _sigmoid(w0_ref[...] + _dot(jnp.tanh(wd).astype(BF), w2_ref[...]))
    a = _sigmoid(a0_ref[...] + _dot(ad.astype(BF), a2_ref[...]))
    g = _dot(_sigmoid(gd).astype(BF), g2_ref[...])
    kq = k * kk_ref[...]
    ssq = _seg_sum(kq * kq, _seg_ones(RWKV_HD))
    kk = kq * lax.rsqrt(jnp.maximum(ssq, 1e-24))
    r_o[...] = r.astype(BF)
    k_o[...] = k.astype(BF)
    v_o[...] = v.astype(BF)
    kk_o[...] = kk.astype(BF)
    a0_o[...] = a[:, :w].astype(BF)
    a1_o[...] = a[:, w:].astype(BF)
    lw0_o[...] = lw[:, :w]
    lw1_o[...] = lw[:, w:]
    g_o[...] = g.astype(BF)


def rwkv_prep(z, p, dims):
    rows = z.shape[0]
    tt = _row_tile(256, dims, within_seq=True)
    hb = SUBLANES_BF16
    nhb = rows // hb
    w = RWKV_WIDTH
    full = lambda shape: pl.BlockSpec(shape, lambda i: (0,) * len(shape))
    out_bf = jax.ShapeDtypeStruct((rows, w), BF)
    out_f = jax.ShapeDtypeStruct((rows, w), F32)
    ospec = pl.BlockSpec((tt, w), lambda i: (i, 0))
    return pl.pallas_call(
        functools.partial(_rwkv_prep_kernel, dims=dims),
        grid=(rows // tt,),
        in_specs=[pl.BlockSpec((tt, RWKV_PAD), lambda i: (i, 0)),
                  pl.BlockSpec((hb, RWKV_PAD), lambda i: (jnp.maximum(i * (tt // hb) - 1, 0), 0)),
                  pl.BlockSpec((hb, RWKV_PAD), lambda i: (jnp.minimum((i + 1) * (tt // hb), nhb - 1), 0)),
                  full((1, RWKV_PAD)), full((1, RWKV_PAD)),
                  full((1, 2 * w)), full((2 * DECAY_LORA, 2 * w)),
                  full((1, 2 * w)), full((2 * ICLR_LORA, 2 * w)),
                  full((GATE_PAD, w)), full((1, w))],
        out_specs=[ospec] * 9,
        out_shape=[out_bf, out_bf, out_bf, out_bf, out_bf, out_bf, out_f, out_f, out_bf],
        compiler_params=_cparams(1),
        name="rwkv_prep",
    )(z, z, z, p['mu_prev'], p['mu_next'], p['w0'], p['w2'], p['a0'], p['a2'], p['g2'], p['k_k'])


def _bmm(a, b):
    return lax.dot_general(a.astype(BF), b.astype(BF), (((2,), (1,)), ((0,), (0,))), preferred_element_type=F32)


def _bmm_nt(a, b):
    return lax.dot_general(a.astype(BF), b.astype(BF), (((2,), (2,)), ((0,), (0,))), preferred_element_type=F32)


def _tri_inverse(lb):
    c = lb.shape[-1]
    row = lax.broadcasted_iota(jnp.int32, (c, c), 0)
    col = lax.broadcasted_iota(jnp.int32, (c, c), 1)
    joins = lambda m: ((row // (2 * m)) == (col // (2 * m))) & ((row // m) != (col // m))
    t = jnp.where(row == col, 1.0, 0.0) - jnp.where(joins(1), lb, 0.0)
    m = 2
    while m < c:
        t = t - _bmm(_bmm(t, jnp.where(joins(m), lb, 0.0)), t)
        m *= 2
    return t


def _head_stack(x, head_of_lane):
    heads = x.shape[1] // RWKV_HD
    return jnp.concatenate([jnp.where(head_of_lane == h, x, 0.0) for h in range(heads)], axis=0).astype(BF)


def _scan_group_operands(r, k, v, kk, a, lw, ka, s_bd, reverse):
    c, wdt = r.shape
    heads = wdt // RWKV_HD
    row = lax.broadcasted_iota(jnp.int32, (c, c), 0)
    col = lax.broadcasted_iota(jnp.int32, (c, c), 1)
    before = (col > row) if reverse else (col < row)
    upto = (col >= row) if reverse else (col <= row)
    kmod = k * (1.0 + (a - 1.0) * ka)
    b = kk * a
    cum = _dot_exact_lhs(jnp.where(upto, 1.0, 0.0).astype(BF), lw)
    tot = jnp.sum(lw, axis=0, keepdims=True)
    g_in, g_ex, g_inv, g_end = jnp.exp(cum), jnp.exp(cum - lw), jnp.exp(-cum), jnp.exp(tot - cum)
    kap = kk * g_ex
    r_t = r * g_in
    km_t = (kmod * g_inv).astype(BF)
    b_t = (b * g_inv).astype(BF)
    head_of_lane = lax.broadcasted_iota(jnp.int32, (c, wdt), 1) // RWKV_HD
    lhs = jnp.concatenate([_head_stack(kap, head_of_lane), _head_stack(r_t, head_of_lane)], axis=0)
    p = _dot_nt(lhs, jnp.concatenate([km_t, b_t], axis=0)).reshape(2 * heads, c, 2 * c)
    row2 = lax.broadcasted_iota(jnp.int32, (c, 2 * c), 0)
    col2 = lax.broadcasted_iota(jnp.int32, (c, 2 * c), 1) % c
    before2 = (col2 > row2) if reverse else (col2 < row2)
    upto2 = (col2 >= row2) if reverse else (col2 <= row2)
    lk_lb = jnp.where(before2, p[:heads], 0.0)
    kap_b = jnp.broadcast_to(kap.astype(BF)[None], (heads, c, wdt))
    r_b = jnp.broadcast_to(r_t.astype(BF)[None], (heads, c, wdt))
    return {
        'lb': lk_lb[:, :, c:],
        'x_rhs': jnp.concatenate([kap_b, lk_lb[:, :, :c].astype(BF)], axis=2),
        'y_rhs': jnp.concatenate([r_b, jnp.where(upto2, p[heads:], 0.0).astype(BF)], axis=2),
        's_rows': s_bd.astype(BF).reshape(heads, RWKV_HD, wdt),
        'vt': v.T.reshape(heads, RWKV_HD, c),
        'kb_e': jnp.concatenate([(kmod * g_end).astype(BF), (b * g_end).astype(BF)], axis=0),
        'g_tot': jnp.exp(tot),
    }


def _rwkv_scan_kernel(rf, kf, vf, kkf, af, lwf, rr, kr, vr, kkr, ar, lwr, ka_ref, y0_ref, y1_ref, s_ref):
    @pl.when(pl.program_id(2) == 0)
    def _():
        s_ref[...] = jnp.zeros(s_ref.shape, F32)

    gw = s_ref.shape[-1]
    heads = gw // RWKV_HD
    n_groups = s_ref.shape[1]
    c = rf.shape[0]
    dirs = ((rf, kf, vf, kkf, af, lwf, False), (rr, kr, vr, kkr, ar, lwr, True))
    groups = []
    for d, (r_ref, k_ref, v_ref, kk_ref, a_ref, lw_ref, reverse) in enumerate(dirs):
        for q in range(n_groups):
            sl = slice(q * gw, (q + 1) * gw)
            ld = lambda ref: ref[:, sl].astype(F32)
            s_bd = s_ref[d, q]
            ops = _scan_group_operands(ld(r_ref), ld(k_ref), ld(v_ref), ld(kk_ref), ld(a_ref), lw_ref[:, sl],
                                       ka_ref[:, sl], s_bd, reverse)
            ops['s_bd'] = s_bd
            groups.append(ops)

    cat = lambda key: jnp.concatenate([g[key] for g in groups], axis=0)
    lb, x_rhs, y_rhs, s_rows, vt = (cat(key) for key in ('lb', 'x_rhs', 'y_rhs', 's_rows', 'vt'))
    vtb = vt.astype(BF)
    xt = _bmm_nt(jnp.concatenate([s_rows, vtb], axis=2), x_rhs)
    t0 = _tri_inverse(lb)
    u0b = _bmm_nt(xt, t0).astype(BF)
    u0 = u0b.astype(F32)
    lb_hi = lb.astype(BF)
    lb_lo = (lb - lb_hi.astype(F32)).astype(BF)
    res = xt - u0 - (_bmm_nt(u0b, lb_hi) + _bmm_nt(u0b, lb_lo))
    ut = u0 + _bmm_nt(res, t0)
    nutb = (-ut).astype(BF)
    yt = _bmm_nt(jnp.concatenate([s_rows, vtb, nutb], axis=2), y_rhs)

    row_head = lax.broadcasted_iota(jnp.int32, (gw, gw), 0) // RWKV_HD
    col_head = lax.broadcasted_iota(jnp.int32, (gw, gw), 1) // RWKV_HD
    for i, g in enumerate(groups):
        d, q = divmod(i, n_groups)
        sl = slice(q * gw, (q + 1) * gw)
        hs = slice(i * heads, (i + 1) * heads)
        y = yt[hs].reshape(gw, c).T
        (y0_ref, y1_ref)[d][:, sl] = y
        vu = jnp.concatenate([vtb[hs].reshape(gw, c), nutb[hs].reshape(gw, c)], axis=1)
        upd = g['s_bd'] * g['g_tot'] + _dot(vu, g['kb_e'])
        s_ref[d, q] = jnp.where(row_head == col_head, upd, 0.0)


def rwkv_scan(r, k, v, kk, a0, a1, lw0, lw1, k_a, dims):
    c = SCAN_CHUNK
    ncc, nlc = dims.cl // c, dims.s // c
    lat0 = dims.r_ctx // c
    gw = SCAN_GROUP_HEADS * RWKV_HD
    wdt = SCAN_GROUPS * gw

    def fwd(b, hb, t):
        return (jnp.where(t < ncc, b * ncc + t, lat0 + b * nlc + (t - ncc)), hb)

    def rev(b, hb, t):
        return (jnp.where(t < ncc, b * ncc + (ncc - 1 - t), lat0 + b * nlc + (nlc - 1 - (t - ncc))), hb)

    fs = pl.BlockSpec((c, wdt), fwd)
    rs = pl.BlockSpec((c, wdt), rev)
    y_shape = jax.ShapeDtypeStruct(lw0.shape, F32)
    return pl.pallas_call(
        _rwkv_scan_kernel,
        grid=(dims.nb, RWKV_WIDTH // wdt, ncc + nlc),
        in_specs=[fs] * 6 + [rs] * 6 + [pl.BlockSpec((1, wdt), lambda b, hb, t: (0, hb))],
        out_specs=[fs, rs],
        out_shape=[y_shape, y_shape],
        scratch_shapes=[pltpu.VMEM((2, SCAN_GROUPS, gw, gw), F32)],
        compiler_params=_cparams(3),
        name="rwkv_scan",
    )(r, k, v, kk, a0, lw0, r, k, v, kk, a1, lw1, k_a)


def _rwkv_out_kernel(y0_ref, y1_ref, r_ref, k_ref, v_ref, a0_ref, a1_ref, g_ref, ka_ref, rk_ref, lng_ref, lnb_ref,
                     o_ref):
    seg = _seg_ones(RWKV_HD)
    y = y0_ref[...] + y1_ref[...]
    mu = _seg_sum(y, seg) * (1.0 / RWKV_HD)
    yc = y - mu
    var = _seg_sum(yc * yc, seg) * (1.0 / RWKV_HD)
    yn = yc * lax.rsqrt(var + RWKV_GN_EPS) * lng_ref[...] + lnb_ref[...]
    ld = lambda ref: ref[...].astype(F32)
    r, k, v = ld(r_ref), ld(k_ref), ld(v_ref)
    kmod_sum = k * (2.0 + (ld(a0_ref) + ld(a1_ref) - 2.0) * ka_ref[...])
    bonus = _seg_sum(r * kmod_sum * rk_ref[...], seg) * v
    o_ref[...] = ((yn + bonus) * ld(g_ref)).astype(BF)


def rwkv_out(y0, y1, r, k, v, a0, a1, g, p, dims, row0=0):
    rows, w = y0.shape
    tr = _row_tile(256, dims)
    b0 = row0 // tr
    rs = pl.BlockSpec((tr, w), lambda i: (i + b0, 0))
    ps = pl.BlockSpec((1, w), lambda i: (0, 0))
    return pl.pallas_call(
        _rwkv_out_kernel,
        grid=((rows - row0) // tr,),
        in_specs=[rs] * 8 + [ps] * 4,
        out_specs=pl.BlockSpec((tr, w), lambda i: (i, 0)),
        out_shape=jax.ShapeDtypeStruct((rows - row0, w), BF),
        compiler_params=_cparams(1),
        name="rwkv_out",
    )(y0, y1, r, k, v, a0, a1, g, p['k_a'], p['r_k'], p['ln_g'], p['ln_b'])


def _conv_kernel(val_ref, gate_ref, vp_ref, gp_ref, vn_ref, gn_ref, w_ref, b_ref, lng_ref, lnb_ref, o_ref,
                 u_ref, acc_ref, *, dims, b0):
    i = pl.program_id(0) + b0
    tt = val_ref.shape[0]
    hb = vp_ref.shape[0]
    pos, cnt = _seq_pos(i, tt, dims)
    glu = lambda a, b: a.astype(F32) * _sigmoid(b.astype(F32))
    u_ref[0:hb, :] = jnp.where(pos > 0, glu(vp_ref[...], gp_ref[...]), 0.0)
    u_ref[hb:hb + tt, :] = glu(val_ref[...], gate_ref[...])
    u_ref[hb + tt:2 * hb + tt, :] = jnp.where(pos < cnt - 1, glu(vn_ref[...], gn_ref[...]), 0.0)
    half = CONV_K // 2
    for c0 in range(0, CONV_CH, LANES):
        acc = jnp.zeros((tt, LANES), F32) + b_ref[:, c0:c0 + LANES]
        for j in range(CONV_K):
            off = hb - half + j
            acc = acc + u_ref[off:off + tt, c0:c0 + LANES] * w_ref[j:j + 1, c0:c0 + LANES]
        acc_ref[:, c0:c0 + LANES] = acc
    y = acc_ref[...]
    mu = jnp.mean(y, axis=-1, keepdims=True)
    yc = y - mu
    var = jnp.mean(yc * yc, axis=-1, keepdims=True)
    yn = yc * lax.rsqrt(var + 1e-5) * lng_ref[...] + lnb_ref[...]
    o_ref[...] = (yn * _sigmoid(yn)).astype(BF)


def conformer_conv(z, p, dims, row0=0):
    rows = z.shape[0]
    tt = _row_tile(256, dims, within_seq=True)
    hb = SUBLANES_BF16
    nhb = rows // hb
    b0 = row0 // tt
    vcol, gcol = Z_CONV // CONV_CH, Z_CONV // CONV_CH + 1
    prev = lambda i: jnp.maximum((i + b0) * (tt // hb) - 1, 0)
    nxt = lambda i: jnp.minimum((i + b0 + 1) * (tt // hb), nhb - 1)
    ps = pl.BlockSpec((1, CONV_CH), lambda i: (0, 0))
    return pl.pallas_call(
        functools.partial(_conv_kernel, dims=dims, b0=b0),
        grid=((rows - row0) // tt,),
        in_specs=[pl.BlockSpec((tt, CONV_CH), lambda i: (i + b0, vcol)),
                  pl.BlockSpec((tt, CONV_CH), lambda i: (i + b0, gcol)),
                  pl.BlockSpec((hb, CONV_CH), lambda i: (prev(i), vcol)),
                  pl.BlockSpec((hb, CONV_CH), lambda i: (prev(i), gcol)),
                  pl.BlockSpec((hb, CONV_CH), lambda i: (nxt(i), vcol)),
                  pl.BlockSpec((hb, CONV_CH), lambda i: (nxt(i), gcol)),
                  pl.BlockSpec((CONV_K + 1, CONV_CH), lambda i: (0, 0)),
                  ps, ps, ps],
        out_specs=pl.BlockSpec((tt, CONV_CH), lambda i: (i, 0)),
        out_shape=jax.ShapeDtypeStruct((rows - row0, CONV_CH), BF),
        scratch_shapes=[pltpu.VMEM((tt + 2 * hb, CONV_CH), F32), pltpu.VMEM((tt, CONV_CH), F32)],
        compiler_params=_cparams(1),
        name="conformer_conv",
    )(z, z, z, z, z, z, p['conv_w'], p['conv_b'], p['conv_ln_g'], p['conv_ln_b'])


def _out_proj_kernel(*refs, n_ctx_blocks):
    if n_ctx_blocks:
        ac_ref, al_ref, r_ref, c_ref, wa_ref, wr_ref, wc_ref, x_ref, g_ref, o_ref = refs
        a = jnp.where(pl.program_id(0) < n_ctx_blocks, ac_ref[...], al_ref[...])
    else:
        al_ref, r_ref, c_ref, wa_ref, wr_ref, wc_ref, x_ref, g_ref, o_ref = refs
        a = al_ref[...]
    acc = _dot(a, wa_ref[...]) + _dot(r_ref[...], wr_ref[...]) + _dot(c_ref[...], wc_ref[...])
    o_ref[...] = x_ref[...] + g_ref[0] * acc


def out_proj(a_ctx, a_lat, rw, cv, w_out, x, gate, dims, row0=0):
    rows, d = x.shape
    tm = _row_tile(1024, dims)
    tn = 512
    b0 = row0 // tm
    nctx = dims.r_ctx // tm
    n_ctx_blocks = nctx - b0
    assert (a_ctx is not None) == (n_ctx_blocks > 0)
    a_specs, a_args = [], []
    if n_ctx_blocks:
        a_specs.append(pl.BlockSpec((tm, Q_COLS), lambda i, j: (jnp.minimum(i, nctx - 1), 0)))
        a_args.append(a_ctx)
    a_specs.append(pl.BlockSpec((tm, Q_COLS), lambda i, j: (jnp.maximum(i + b0 - nctx, 0), 0)))
    a_args.append(a_lat)
    return pl.pallas_call(
        functools.partial(_out_proj_kernel, n_ctx_blocks=n_ctx_blocks),
        grid=((rows - row0) // tm, d // tn),
        in_specs=a_specs + [
                  pl.BlockSpec((tm, RWKV_WIDTH), lambda i, j: (i, 0)),
                  pl.BlockSpec((tm, CONV_CH), lambda i, j: (i, 0)),
                  pl.BlockSpec((Q_COLS, tn), lambda i, j: (0, j)),
                  pl.BlockSpec((RWKV_WIDTH, tn), lambda i, j: (0, j)),
                  pl.BlockSpec((CONV_CH, tn), lambda i, j: (0, j)),
                  pl.BlockSpec((tm, tn), lambda i, j: (i + b0, j)),
                  pl.BlockSpec((1, 1, tn), lambda i, j: (_grp(i + b0, tm, dims), 0, j))],
        out_specs=pl.BlockSpec((tm, tn), lambda i, j: (i + b0, j)),
        out_shape=jax.ShapeDtypeStruct(x.shape, F32),
        input_output_aliases={len(a_args) + 5: 0},
        compiler_params=_cparams(2),
        name="out_proj",
    )(*a_args, rw, cv, w_out[:Q_COLS], w_out[Q_COLS:Q_COLS + RWKV_WIDTH], w_out[Q_COLS + RWKV_WIDTH:], x, gate)


def _row_gather(idx_ref, n, src_hbm, dst, sem):
    def body(r, carry):
        t = idx_ref[0, 0, r]
        pltpu.make_async_copy(src_hbm.at[pl.ds(t, 1)], dst.at[pl.ds(r, 1)], sem).start()
        return carry
    lax.fori_loop(0, n, body, 0, unroll=8)


def _moe_kernel(be_ref, act_ref, tok_ref, tokn_ref, h_hbm, wg_ref, wu_ref, wd_ref, o_ref, xbuf, sem):
    del be_ref
    i = pl.program_id(0)
    slot = i % 2
    rows = xbuf.shape[1]

    @pl.when(i == 0)
    def _():
        _row_gather(tok_ref, rows, h_hbm, xbuf.at[0], sem.at[0])

    @pl.when((i == 0) | (act_ref[jnp.maximum(i - 1, 0)] > 0))
    def _():
        pltpu.make_async_copy(h_hbm.at[pl.ds(0, rows)], xbuf.at[slot], sem.at[slot]).wait()

    @pl.when(act_ref[i] > 0)
    def _():
        for r in range(rows):
            pltpu.make_async_copy(h_hbm.at[pl.ds(tokn_ref[0, 0, r], 1)], xbuf.at[1 - slot, pl.ds(r, 1)],
                                  sem.at[1 - slot]).start()
        xb = xbuf[slot].astype(BF)
        gate = _dot(xb, wg_ref[0])
        up = _dot(xb, wu_ref[0])
        hid = (gate * _sigmoid(gate) * up).astype(BF)
        o_ref[...] = _dot(hid, wd_ref[0])

    @pl.when(act_ref[i] == 0)
    def _():
        o_ref[...] = jnp.zeros(o_ref.shape, F32)


def moe_experts(h2, blk_expert, blk_active, slot_tok, w_gate, w_up, w_down):
    n_blk = blk_expert.shape[0]
    d = h2.shape[1]
    de = w_gate.shape[2]
    b = MOE_ROWS
    tok3 = slot_tok.reshape(n_blk, 1, b)
    grid_spec = pltpu.PrefetchScalarGridSpec(
        num_scalar_prefetch=2,
        grid=(n_blk,),
        in_specs=[pl.BlockSpec((1, 1, b), lambda i, be, act: (i, 0, 0), memory_space=pltpu.SMEM),
                  pl.BlockSpec((1, 1, b), lambda i, be, act: (jnp.minimum(i + 1, n_blk - 1), 0, 0),
                               memory_space=pltpu.SMEM),
                  pl.BlockSpec(memory_space=pl.ANY),
                  pl.BlockSpec((1, d, de), lambda i, be, act: (be[i], 0, 0)),
                  pl.BlockSpec((1, d, de), lambda i, be, act: (be[i], 0, 0)),
                  pl.BlockSpec((1, de, d), lambda i, be, act: (be[i], 0, 0))],
        out_specs=pl.BlockSpec((b, d), lambda i, be, act: (i, 0)),
        scratch_shapes=[pltpu.VMEM((2, b, d), F32), pltpu.SemaphoreType.DMA((2,))],
    )
    return pl.pallas_call(
        _moe_kernel,
        grid_spec=grid_spec,
        out_shape=jax.ShapeDtypeStruct((n_blk * b, d), F32),
        compiler_params=_cparams(1),
        name="moe_experts",
    )(blk_expert, blk_active, tok3, tok3, h2, w_gate, w_up, w_down)


def _combine_kernel(pos_ref, posn_ref, o_hbm, wt_ref, x_ref, g_ref, out_ref, buf, sem):
    i = pl.program_id(0)
    nblk = pl.num_programs(0)
    slot = i % 2
    n2 = buf.shape[1]

    @pl.when(i == 0)
    def _():
        _row_gather(pos_ref, n2, o_hbm, buf.at[0], sem.at[0])

    @pl.when(i + 1 < nblk)
    def _():
        _row_gather(posn_ref, n2, o_hbm, buf.at[1 - slot], sem.at[1 - slot])

    pltpu.make_async_copy(o_hbm.at[pl.ds(0, n2)], buf.at[slot], sem.at[slot]).wait()
    tb = n2 // TOP_K
    wt = wt_ref[...]
    y = buf[slot, 0:tb, :] * wt[:, 0:1] + buf[slot, tb:n2, :] * wt[:, 1:2]
    out_ref[...] = x_ref[...] + g_ref[0] * y


def moe_combine(o_slots, pos, wts, x, gate, dims, row0=0, drop_head_rows=False):
    rows, d = x.shape
    tb = COMBINE_ROWS
    n_blk = (rows - row0) // tb
    b0 = row0 // tb
    out_row0 = row0 if drop_head_rows else 0
    ob0 = b0 - out_row0 // tb
    posb = pos.reshape(n_blk, tb, TOP_K).transpose(0, 2, 1).reshape(n_blk, 1, TOP_K * tb)
    return pl.pallas_call(
        _combine_kernel,
        grid=(n_blk,),
        in_specs=[pl.BlockSpec((1, 1, TOP_K * tb), lambda i: (i, 0, 0), memory_space=pltpu.SMEM),
                  pl.BlockSpec((1, 1, TOP_K * tb), lambda i: (jnp.minimum(i + 1, n_blk - 1), 0, 0),
                               memory_space=pltpu.SMEM),
                  pl.BlockSpec(memory_space=pl.ANY),
                  pl.BlockSpec((tb, TOP_K), lambda i: (i, 0)),
                  pl.BlockSpec((tb, d), lambda i: (i + b0, 0)),
                  pl.BlockSpec((1, 1, d), lambda i: (_grp(i + b0, tb, dims), 0, 0))],
        out_specs=pl.BlockSpec((tb, d), lambda i: (i + ob0, 0)),
        out_shape=jax.ShapeDtypeStruct((rows - out_row0, d), F32),
        scratch_shapes=[pltpu.VMEM((2, TOP_K * tb, d), F32), pltpu.SemaphoreType.DMA((2,))],
        input_output_aliases={} if drop_head_rows else {4: 0},
        compiler_params=_cparams(1),
        name="moe_combine",
    )(posb, posb, o_slots, wts, x, gate)


def _top2(v):
    assert TOP_K == 2
    idx = lax.broadcasted_iota(jnp.int32, v.shape, v.ndim - 1)
    i1 = jnp.argmax(v, axis=-1)
    rest = jnp.where(idx == i1[..., None], -jnp.inf, v)
    i2 = jnp.argmax(rest, axis=-1)
    vals = jnp.stack([jnp.max(v, axis=-1), jnp.max(rest, axis=-1)], axis=-1)
    return vals, jnp.stack([i1, i2], axis=-1).astype(jnp.int32)


def moe_routing(logits, router_b):
    n = logits.shape[0]
    scores = jax.nn.sigmoid(logits[:, :N_EXPERTS])
    sel = (scores + router_b.astype(F32)).reshape(n, N_GROUPS, EXPERTS_PER_GROUP)
    grp_score = jnp.sum(_top2(sel)[0], axis=-1)
    g_idx = jnp.argmax(grp_score, axis=-1)
    in_grp = jnp.take_along_axis(sel, g_idx[:, None, None], axis=1)[:, 0]
    _, local = _top2(in_grp)
    expert = g_idx[:, None] * EXPERTS_PER_GROUP + local
    wts = jnp.take_along_axis(scores, expert, axis=1)
    wts = wts / jnp.sum(wts, axis=-1, keepdims=True)

    b = MOE_ROWS
    nk = n * TOP_K
    e_flat = expert.reshape(-1).astype(jnp.int32)
    pos_flat, base = expert_slots(e_flat)
    pend = base[1:N_EXPERTS + 1]
    n_blocks = -(-(nk + N_EXPERTS * (b - 1)) // b) + 1
    tok = jnp.arange(nk, dtype=jnp.int32) // TOP_K
    slot_tok = jnp.zeros((n_blocks * b,), jnp.int32).at[pos_flat].set(tok)
    blk_start = jnp.arange(n_blocks, dtype=jnp.int32) * b
    blk_expert = jnp.minimum(jnp.sum((pend[None, :] <= blk_start[:, None]).astype(jnp.int32), axis=1),
                             N_EXPERTS - 1)
    blk_active = (blk_start < pend[-1]).astype(jnp.int32)
    return blk_expert, blk_active, slot_tok, wts, pos_flat.reshape(n, TOP_K)


def _expert_slots_kernel(e_ref, pos_ref, base_ref, cnt_ref, *, block_rows):
    ph = pl.program_id(0)
    j = pl.program_id(1)
    e = e_ref[0]
    tb = e.shape[1]
    onehot = jnp.where(lax.broadcasted_iota(jnp.int32, (LANES, tb), 0) == e, 1.0, 0.0)
    in_block = jnp.sum(onehot, axis=1, keepdims=True)

    @pl.when((ph == 0) & (j == 0))
    def _():
        cnt_ref[...] = jnp.zeros(cnt_ref.shape, F32)

    @pl.when(ph == 0)
    def _():
        cnt_ref[...] += in_block

    @pl.when((ph == 1) & (j == 0))
    def _():
        padded = jnp.floor((cnt_ref[...] + (block_rows - 1)) * (1.0 / block_rows)) * block_rows
        row = lax.broadcasted_iota(jnp.int32, (LANES, LANES), 0)
        col = lax.broadcasted_iota(jnp.int32, (LANES, LANES), 1)
        earlier = jnp.where(col < row, 1.0, 0.0).astype(BF)
        base_ref[...] = _dot_exact_lhs(earlier, jnp.broadcast_to(padded, (LANES, LANES)))[:, 0:1]
        cnt_ref[...] = jnp.zeros(cnt_ref.shape, F32)

    @pl.when(ph == 1)
    def _():
        row = lax.broadcasted_iota(jnp.int32, (tb, tb), 0)
        col = lax.broadcasted_iota(jnp.int32, (tb, tb), 1)
        before = jnp.where(row < col, 1.0, 0.0).astype(BF)
        rank = _dot(onehot.astype(BF), before)
        slot = jnp.sum(onehot * (base_ref[...] + cnt_ref[...] + rank), axis=0, keepdims=True)
        pos_ref[0] = slot.astype(jnp.int32)
        cnt_ref[...] += in_block


def expert_slots(e_flat):
    nk = e_flat.shape[0]
    tb = math.gcd(nk, 512)
    n_blk = nk // tb
    pos, base = pl.pallas_call(
        functools.partial(_expert_slots_kernel, block_rows=MOE_ROWS),
        grid=(2, n_blk),
        in_specs=[pl.BlockSpec((1, 1, tb), lambda ph, j: (j, 0, 0))],
        out_specs=[pl.BlockSpec((1, 1, tb), lambda ph, j: (ph * j, 0, 0)),
                   pl.BlockSpec((LANES, 1), lambda ph, j: (0, 0))],
        out_shape=[jax.ShapeDtypeStruct((n_blk, 1, tb), jnp.int32), jax.ShapeDtypeStruct((LANES, 1), F32)],
        scratch_shapes=[pltpu.VMEM((LANES, 1), F32)],
        compiler_params=_cparams(2),
        name="expert_slots",
    )(e_flat.reshape(n_blk, 1, tb))
    return pos.reshape(nk), base[:, 0].astype(jnp.int32)


def _rope_tables(dims):
    rows = dims.s // GRID_W
    row = jnp.repeat(jnp.arange(rows), GRID_W).astype(F32)
    col = jnp.tile(jnp.arange(GRID_W), rows).astype(F32)
    inv = ROPE_BASE ** (-jnp.arange(ROPE_PAIRS, dtype=F32) / ROPE_PAIRS)
    ang = jnp.concatenate([row[:, None] * inv, row[:, None] * inv, col[:, None] * inv, col[:, None] * inv], axis=1)
    sign = jnp.tile(jnp.concatenate([-jnp.ones((ROPE_PAIRS,), F32), jnp.ones((ROPE_PAIRS,), F32)]), 2)
    cos_t = jnp.concatenate([jnp.ones((dims.cl, ATT_HD), F32), jnp.cos(ang)], axis=0)
    sin_t = jnp.concatenate([jnp.zeros((dims.cl, ATT_HD), F32), jnp.sin(ang) * sign], axis=0)
    return cos_t, sin_t


def _block_diag2(w):
    z = jnp.zeros_like(w[0])
    return jnp.concatenate([jnp.concatenate([w[0], z], axis=1), jnp.concatenate([z, w[1]], axis=1)], axis=0)


def _pad_cols(v, n):
    return jnp.pad(v, [(0, 0)] * (v.ndim - 1) + [(0, n - v.shape[-1])])


def _layer_params(l, w_in, q_norm_g, k_norm_g, attn_sink, rwkv_mu_prev, rwkv_mu_next, rwkv_w0, rwkv_w2, rwkv_a0,
                  rwkv_a2, rwkv_g2, rwkv_k_k, rwkv_k_a, rwkv_r_k, rwkv_ln_g, rwkv_ln_b, conv_w, conv_b, conv_ln_g,
                  conv_ln_b, dims):
    d = w_in.shape[1]
    wl = w_in[l]
    w_in_p = jnp.concatenate([wl[:, ATT_COLS:ATT_COLS + RWKV_COLS], jnp.zeros((d, RWKV_PAD - RWKV_COLS), F32),
                              wl[:, ATT_COLS + RWKV_COLS:], wl[:, :ATT_COLS]], axis=1).astype(BF)
    scale = ATT_HD ** -0.5
    gains = jnp.concatenate([jnp.tile(q_norm_g[l] * scale, (ATT_HEADS, 1)), jnp.tile(k_norm_g[l], (KV_HEADS, 1))],
                            axis=0).reshape(1, Q_COLS + KV_COLS)
    sink = attn_sink[l].astype(F32).reshape(KV_HEADS, KV_GROUP, 1)
    return {
        'w_in': w_in_p,
        'gains': gains,
        'sink_lat': jnp.repeat(sink, ATT_BLOCK, axis=1).reshape(KV_HEADS, KV_GROUP * ATT_BLOCK, 1),
        'sink_ctx': jnp.repeat(sink, dims.cl, axis=1).reshape(KV_HEADS, KV_GROUP * dims.cl, 1),
        'mu_prev': _pad_cols(rwkv_mu_prev[l][None], RWKV_PAD),
        'mu_next': _pad_cols(rwkv_mu_next[l][None], RWKV_PAD),
        'w0': rwkv_w0[l].reshape(1, 2 * RWKV_WIDTH),
        'w2': _block_diag2(rwkv_w2[l]).astype(BF),
        'a0': rwkv_a0[l].reshape(1, 2 * RWKV_WIDTH),
        'a2': _block_diag2(rwkv_a2[l]).astype(BF),
        'g2': jnp.pad(rwkv_g2[l], ((0, GATE_PAD - GATE_LORA), (0, 0))).astype(BF),
        'k_k': rwkv_k_k[l][None],
        'k_a': rwkv_k_a[l][None],
        'r_k': rwkv_r_k[l].reshape(1, RWKV_WIDTH),
        'ln_g': rwkv_ln_g[l][None],
        'ln_b': rwkv_ln_b[l][None],
        'conv_w': jnp.pad(conv_w[l], ((0, 1), (0, 0))),
        'conv_b': conv_b[l][None],
        'conv_ln_g': conv_ln_g[l][None],
        'conv_ln_b': conv_ln_b[l][None],
    }


def kernel(x, c, ctx, c_ctx, ada_w, ada_b, norm1_g, norm2_g, w_in, q_norm_g, k_norm_g, attn_sink, rwkv_mu_prev, rwkv_mu_next, rwkv_w0, rwkv_w2, rwkv_a0, rwkv_a2, rwkv_g2, rwkv_k_k, rwkv_k_a, rwkv_r_k, rwkv_ln_g, rwkv_ln_b, conv_w, conv_b, conv_ln_g, conv_ln_b, w_out, router_w, router_b, moe_w_gate, moe_w_up, moe_w_down):
    nb, s, d = x.shape
    dims = Dims(nb=nb, s=s, cl=ctx.shape[1])
    n_layers = w_in.shape[0]
    mod_rows = -(-(nb + 1) // 8) * 8
    cs = jnp.zeros((mod_rows, d), F32).at[:nb].set(c).at[nb].set(c_ctx)
    mods = ada_tables(cs, ada_w, ada_b).reshape(n_layers, mod_rows, 6, 1, d)
    cos_t, sin_t = _rope_tables(dims)
    router_p = _pad_cols(router_w.astype(F32), LANES)
    n_experts = moe_w_gate.shape[1]
    stack = lambda w: w.astype(BF).reshape((n_layers * n_experts,) + w.shape[2:])
    wg_all, wu_all, wd_all = stack(moe_w_gate), stack(moe_w_up), stack(moe_w_down)
    xa = jnp.concatenate([ctx.reshape(dims.r_ctx, d), x.reshape(dims.r_lat, d)], axis=0)

    for l in range(n_layers):
        ctx_out = l < n_layers - 1
        row0 = 0 if ctx_out else dims.r_ctx
        p = _layer_params(l, w_in, q_norm_g, k_norm_g, attn_sink, rwkv_mu_prev, rwkv_mu_next, rwkv_w0, rwkv_w2,
                          rwkv_a0, rwkv_a2, rwkv_g2, rwkv_k_k, rwkv_k_a, rwkv_r_k, rwkv_ln_g, rwkv_ln_b, conv_w,
                          conv_b, conv_ln_g, conv_ln_b, dims)
        sh1, sc1, g1, sh2, sc2, g2 = [mods[l, :, m] for m in range(6)]

        h1 = norm_mod(xa, norm1_g[l], sh1, sc1, dims)
        z = in_proj(h1, p['w_in'], dims)

        qk = qk_prep(z, p['gains'], cos_t, sin_t, dims)
        att_lat = attn_latent(qk, z, p['sink_lat'], dims)
        att_ctx = attn_context(qk, z, p['sink_ctx'], dims) if ctx_out else None

        r, k, v, kk, a0, a1, lw0, lw1, g = rwkv_prep(z, p, dims)
        y0, y1 = rwkv_scan(r, k, v, kk, a0, a1, lw0, lw1, p['k_a'], dims)
        rw = rwkv_out(y0, y1, r, k, v, a0, a1, g, p, dims, row0=row0)

        cv = conformer_conv(z, p, dims, row0=row0)
        xa = out_proj(att_ctx, att_lat, rw, cv, w_out[l].astype(BF), xa, g1, dims, row0=row0)

        h2, logits = norm_mod(xa, norm2_g[l], sh2, sc2, dims, row0=row0, router_w=router_p)
        blk_expert, blk_active, slot_tok, wts, pos = moe_routing(logits, router_b)
        o_slots = moe_experts(h2, blk_expert + l * n_experts, blk_active, slot_tok, wg_all, wu_all, wd_all)
        xa = moe_combine(o_slots, pos, wts, xa, g2, dims, row0=row0, drop_head_rows=not ctx_out)

    return xa.reshape(nb, s, d)
```
